```python
import jax, jax.numpy as jnp
from jax import lax
import numpy as np

D_MODEL = 4096
BATCH = 2
SEQ = 8192
DEPTH = 2

N_HEADS = 16
HEAD_DIM = 128
N_KV_HEADS = 4
GROUP = N_HEADS // N_KV_HEADS
D_ATT = N_HEADS * HEAD_DIM
D_KV = N_KV_HEADS * HEAD_DIM
WINDOW = 128
BLOCK = 128
N_BUCKETS = 32
MAX_DISTANCE = 128
D_CONV = D_MODEL // 2
CONV_WIDTH = 3
N_BRANCHES = 2
D_FF = 4 * D_MODEL
N_MOD = 6
EPS = 1e-6
SPLITS = [D_ATT, D_KV, D_KV, D_CONV, D_CONV, D_CONV, N_BRANCHES * D_MODEL]
D_IN = sum(SPLITS)

kernel_name = "hybrid_gated_window_gqa_shortconv_encoder"


def rmsnorm(x, g):
    xf = x.astype(jnp.float32)
    y = xf * lax.rsqrt(jnp.mean(xf * xf, axis=-1, keepdims=True) + EPS)
    return (y * g.astype(jnp.float32)).astype(x.dtype)


def t5_bucket(rel):
    nb = N_BUCKETS // 2
    max_exact = nb // 2
    ret = (rel > 0).astype(jnp.int32) * nb
    n = jnp.abs(rel)
    nf = jnp.maximum(n, 1).astype(jnp.float32)
    large = max_exact + (jnp.log(nf / max_exact) / np.float32(np.log(MAX_DISTANCE / max_exact))
                         * (nb - max_exact)).astype(jnp.int32)
    large = jnp.minimum(large, nb - 1)
    return ret + jnp.where(n < max_exact, n, large)


def window_attention(q, k, v, rel_bias, sink):
    bsz, seq = q.shape[0], q.shape[1]
    nb = seq // BLOCK
    qb = q.reshape(bsz, nb, BLOCK, N_KV_HEADS, GROUP, HEAD_DIM)

    def band(t):
        tp = jnp.pad(t, ((0, 0), (BLOCK, BLOCK), (0, 0), (0, 0)))
        tp = tp.reshape(bsz, nb + 2, BLOCK, N_KV_HEADS, HEAD_DIM)
        return jnp.concatenate([tp[:, 0:nb], tp[:, 1:nb + 1], tp[:, 2:nb + 2]], axis=2)

    kb, vb = band(k), band(v)
    scores = jnp.einsum('bnqkgd,bnskd->bnkgqs', qb, kb).astype(jnp.float32) * np.float32(HEAD_DIM ** -0.5)

    q_idx = jnp.arange(BLOCK)[:, None]
    s_idx = jnp.arange(3 * BLOCK)[None, :]
    rel = s_idx - BLOCK - q_idx
    bias = rel_bias.astype(jnp.float32)[t5_bucket(rel)]
    bias = jnp.transpose(bias, (2, 0, 1)).reshape(N_KV_HEADS, GROUP, BLOCK, 3 * BLOCK)
    key_abs = jnp.arange(nb)[:, None] * BLOCK + s_idx - BLOCK
    in_seq = (key_abs >= 0) & (key_abs < seq)
    valid = (jnp.abs(rel) <= WINDOW)[None] & in_seq[:, None, :]
    scores = jnp.where(valid[None, :, None, None], scores + bias, np.float32(-1e30))

    sink_l = sink.astype(jnp.float32).reshape(N_KV_HEADS, GROUP)[None, None, :, :, None, None]
    m = jnp.maximum(jnp.max(scores, axis=-1, keepdims=True), sink_l)
    p = jnp.exp(scores - m)
    denom = jnp.sum(p, axis=-1, keepdims=True) + jnp.exp(sink_l - m)
    p = (p / denom).astype(v.dtype)
    out = jnp.einsum('bnkgqs,bnskd->bnqkgd', p, vb)
    return out.reshape(bsz, seq, D_ATT)


def short_conv_centred(u, w):
    up = jnp.pad(u, ((0, 0), (1, 1), (0, 0)))
    return w[0] * up[:, :-2] + w[1] * up[:, 1:-1] + w[2] * up[:, 2:]


def setup_inputs(seed: int = 0) -> dict:
    key = jax.random.key(seed)
    ks = jax.random.split(key, 20)
    f32 = jnp.float32

    def nrm(k, shape, scale):
        return jax.random.normal(k, shape, f32) * np.float32(scale)

    return {
        "x": nrm(ks[0], (BATCH, SEQ, D_MODEL), 1.0),
        "c": nrm(ks[1], (BATCH, D_MODEL), 1.0),
        "rel_bias": nrm(ks[2], (N_BUCKETS, N_HEADS), 0.5),
        "norm1_g": 1.0 + nrm(ks[3], (DEPTH, D_MODEL), 0.05),
        "norm2_g": 1.0 + nrm(ks[4], (DEPTH, D_MODEL), 0.05),
        "w_ada": nrm(ks[5], (DEPTH, D_MODEL, N_MOD * D_MODEL), 0.5 * D_MODEL ** -0.5),
        "b_ada": nrm(ks[6], (DEPTH, N_MOD * D_MODEL), 0.02),
        "w_in": nrm(ks[7], (DEPTH, D_MODEL, D_IN), D_MODEL ** -0.5),
        "b_merge": nrm(ks[8], (DEPTH, N_BRANCHES * D_MODEL), 0.02),
        "sink": nrm(ks[9], (DEPTH, N_HEADS), 1.0),
        "conv_w": nrm(ks[10], (DEPTH, CONV_WIDTH, D_CONV), CONV_WIDTH ** -0.5),
        "w_att_proj": nrm(ks[11], (DEPTH, D_ATT, D_MODEL), D_ATT ** -0.5),
        "w_conv_proj": nrm(ks[12], (DEPTH, D_CONV, D_MODEL), D_CONV ** -0.5),
        "w_out": nrm(ks[13], (DEPTH, D_MODEL, D_MODEL), D_MODEL ** -0.5),
        "w_mlp_in": nrm(ks[14], (DEPTH, D_MODEL, D_FF), D_MODEL ** -0.5),
        "w_mlp_out": nrm(ks[15], (DEPTH, D_FF, D_MODEL), D_FF ** -0.5),
        "final_g": 1.0 + nrm(ks[16], (D_MODEL,), 0.05),
    }


def reference(x, c, rel_bias, norm1_g, norm2_g, w_ada, b_ada, w_in, b_merge, sink, conv_w,
              w_att_proj, w_conv_proj, w_out, w_mlp_in, w_mlp_out, final_g):
    bsz, seq = x.shape[0], x.shape[1]
    split_at = [int(s) for s in np.cumsum(SPLITS)[:-1]]
    c_act = jax.nn.silu(c)
    for l in range(DEPTH):
        mod = c_act @ w_ada[l] + b_ada[l]
        shift1, scale1, gate1, shift2, scale2, gate2 = [t[:, None, :] for t in jnp.split(mod, N_MOD, axis=-1)]

        h = rmsnorm(x, norm1_g[l]) * (1 + scale1) + shift1
        z = h @ w_in[l]
        q, k, v, u, gb, gc, glog = jnp.split(z, split_at, axis=-1)
        q = q.reshape(bsz, seq, N_HEADS, HEAD_DIM)
        k = k.reshape(bsz, seq, N_KV_HEADS, HEAD_DIM)
        v = v.reshape(bsz, seq, N_KV_HEADS, HEAD_DIM)
        y_att = window_attention(q, k, v, rel_bias, sink[l]) @ w_att_proj[l]
        y_conv = (gb * short_conv_centred(gc * u, conv_w[l])) @ w_conv_proj[l]
        g_att, g_conv = jnp.split(jax.nn.sigmoid(glog + b_merge[l]), N_BRANCHES, axis=-1)
        merged = g_att * y_att + g_conv * y_conv
        x = x + gate1 * (merged @ w_out[l])

        h2 = rmsnorm(x, norm2_g[l]) * (1 + scale2) + shift2
        a = jnp.square(jax.nn.relu(h2 @ w_mlp_in[l]))
        x = x + gate2 * (a @ w_mlp_out[l])
    return rmsnorm(x, final_g)
```

```python
import functools

import jax
import jax.numpy as jnp
import numpy as np
from jax import lax
from jax.experimental import pallas as pl
from jax.experimental.pallas import tpu as pltpu

F32 = jnp.float32
BF16 = jnp.bfloat16

D_MODEL = 4096
N_HEADS = 16
HEAD_DIM = 128
N_KV_HEADS = 4
GROUP = N_HEADS // N_KV_HEADS
D_ATT = N_HEADS * HEAD_DIM
D_KV = N_KV_HEADS * HEAD_DIM
WINDOW = 128
BLOCK = 128
N_BUCKETS = 32
MAX_DISTANCE = 128
D_CONV = D_MODEL // 2
D_FF = 4 * D_MODEL
N_MOD = 6
EPS = 1e-6
NEG = -1e30

OFF_Q = 0
OFF_K = OFF_Q + D_ATT
OFF_V = OFF_K + D_KV
OFF_U = OFF_V + D_KV
OFF_GB = OFF_U + D_CONV
OFF_GC = OFF_GB + D_CONV
OFF_GA = OFF_GC + D_CONV
OFF_GCV = OFF_GA + D_MODEL
D_IN = OFF_GCV + D_MODEL

VMEM_LIMIT_BYTES = 56 * 1024 * 1024

MM_BM = 1024
MM_BN = 1024
MM_BK = 2048
NORM_ROWS = 256
ATT_ROWS = 512
CONV_ROWS = 512
CONV_COLS = 1024
HALO = 16
MOD_BN = 512
MOD_ROWS = 8


def _params(sem):
    return pltpu.CompilerParams(dimension_semantics=sem, vmem_limit_bytes=VMEM_LIMIT_BYTES)


def _mod_kernel(c_ref, w_ref, b_ref, o_ref):
    c = c_ref[...]
    c_act = c * jax.nn.sigmoid(c)
    acc = jnp.dot(c_act.astype(BF16), w_ref[...].astype(BF16), preferred_element_type=F32)
    o_ref[...] = acc + b_ref[...]


def _modulation(c_pad, w_ada, b_ada):
    depth, d, n = w_ada.shape
    return pl.pallas_call(
        _mod_kernel,
        grid=(depth, n // MOD_BN),
        in_specs=[
            pl.BlockSpec((MOD_ROWS, d), lambda l, j: (0, 0)),
            pl.BlockSpec((None, d, MOD_BN), lambda l, j: (l, 0, j)),
            pl.BlockSpec((None, 1, MOD_BN), lambda l, j: (l, 0, j)),
        ],
        out_specs=pl.BlockSpec((None, MOD_ROWS, MOD_BN), lambda l, j: (l, 0, j)),
        out_shape=jax.ShapeDtypeStruct((depth, MOD_ROWS, n), F32),
        compiler_params=_params(("arbitrary", "arbitrary")),
        name="adaln_mod",
    )(c_pad, w_ada, b_ada.reshape(depth, 1, n))


def _norm_mod_kernel(x_ref, g_ref, scale_ref, shift_ref, o_ref):
    x = x_ref[...]
    y = x * lax.rsqrt(jnp.mean(x * x, axis=-1, keepdims=True) + EPS)
    h = (y * g_ref[...]) * (1.0 + scale_ref[...]) + shift_ref[...]
    o_ref[...] = h.astype(o_ref.dtype)


def _norm_kernel(x_ref, g_ref, o_ref):
    x = x_ref[...]
    y = x * lax.rsqrt(jnp.mean(x * x, axis=-1, keepdims=True) + EPS)
    o_ref[...] = (y * g_ref[...]).astype(o_ref.dtype)


def _norm_mod(x, g, scale, shift, seq):
    m, d = x.shape
    per_batch = seq // NORM_ROWS
    row = pl.BlockSpec((NORM_ROWS, d), lambda i: (i, 0))
    mod = pl.BlockSpec((None, 1, d), lambda i: (i // per_batch, 0, 0))
    return pl.pallas_call(
        _norm_mod_kernel,
        grid=(m // NORM_ROWS,),
        in_specs=[row, pl.BlockSpec((1, d), lambda i: (0, 0)), mod, mod],
        out_specs=row,
        out_shape=jax.ShapeDtypeStruct((m, d), BF16),
        compiler_params=_params(("arbitrary",)),
        name="norm_mod",
    )(x, g.reshape(1, d), scale, shift)


def _final_norm(x, g):
    m, d = x.shape
    row = pl.BlockSpec((NORM_ROWS, d), lambda i: (i, 0))
    return pl.pallas_call(
        _norm_kernel,
        grid=(m // NORM_ROWS,),
        in_specs=[row, pl.BlockSpec((1, d), lambda i: (0, 0))],
        out_specs=row,
        out_shape=jax.ShapeDtypeStruct((m, d), F32),
        compiler_params=_params(("arbitrary",)),
        name="final_norm",
    )(x, g.reshape(1, d))


def _mm_kernel(*refs, npairs, nextra, nk, epilogue):
    a_refs = refs[:npairs]
    b_refs = refs[npairs:2 * npairs]
    extra = refs[2 * npairs:2 * npairs + nextra]
    o_ref = refs[2 * npairs + nextra]
    acc_refs = refs[2 * npairs + nextra + 1:]
    if nk == 1:
        accs = [jnp.dot(a[...], b[...], preferred_element_type=F32) for a, b in zip(a_refs, b_refs)]
        o_ref[...] = epilogue(accs, extra).astype(o_ref.dtype)
        return
    k = pl.program_id(2)
    parts = [jnp.dot(a[...], b[...], preferred_element_type=F32) for a, b in zip(a_refs, b_refs)]

    @pl.when(k == 0)
    def _():
        for acc, part in zip(acc_refs, parts):
            acc[...] = part

    @pl.when(k > 0)
    def _():
        for acc, part in zip(acc_refs, parts):
            acc[...] += part

    @pl.when(k == nk - 1)
    def _():
        o_ref[...] = epilogue([acc[...] for acc in acc_refs], extra).astype(o_ref.dtype)


def _matmul(pairs, extras, epilogue, n_out, out_dtype, name, bk=None):
    m, kdim = pairs[0][0].shape
    bk = kdim if bk is None else bk
    nk = kdim // bk
    grid = (m // MM_BM, n_out // MM_BN) + ((nk,) if nk > 1 else ())
    if nk > 1:
        a_spec = pl.BlockSpec((MM_BM, bk), lambda i, j, k: (i, k))
        b_spec = pl.BlockSpec((bk, MM_BN), lambda i, j, k: (k, j))
        o_spec = pl.BlockSpec((MM_BM, MM_BN), lambda i, j, k: (i, j))
        ex_specs = [pl.BlockSpec(bs, (lambda f: lambda i, j, k: f(i, j))(f)) for _, bs, f in extras]
        sem = ("arbitrary", "arbitrary", "arbitrary")
        scratch = [pltpu.VMEM((MM_BM, MM_BN), F32) for _ in pairs]
    else:
        a_spec = pl.BlockSpec((MM_BM, bk), lambda i, j: (i, 0))
        b_spec = pl.BlockSpec((bk, MM_BN), lambda i, j: (0, j))
        o_spec = pl.BlockSpec((MM_BM, MM_BN), lambda i, j: (i, j))
        ex_specs = [pl.BlockSpec(bs, f) for _, bs, f in extras]
        sem = ("arbitrary", "arbitrary")
        scratch = []
    body = functools.partial(_mm_kernel, npairs=len(pairs), nextra=len(extras), nk=nk, epilogue=epilogue)
    return pl.pallas_call(
        body,
        grid=grid,
        in_specs=[a_spec] * len(pairs) + [b_spec] * len(pairs) + ex_specs,
        out_specs=o_spec,
        out_shape=jax.ShapeDtypeStruct((m, n_out), out_dtype),
        scratch_shapes=scratch,
        compiler_params=_params(sem),
        name=name,
    )(*[a for a, _ in pairs], *[b for _, b in pairs], *[e for e, _, _ in extras])


def _ep_plain(accs, extra):
    return accs[0]


def _ep_relu2(accs, extra):
    return jnp.square(jnp.maximum(accs[0], 0.0))


def _ep_merge(accs, extra):
    glog_att, glog_conv, b_att, b_conv = extra
    g_att = jax.nn.sigmoid(glog_att[...].astype(F32) + b_att[...])
    g_conv = jax.nn.sigmoid(glog_conv[...].astype(F32) + b_conv[...])
    return g_att * accs[0] + g_conv * accs[1]


def _ep_residual(accs, extra):
    x, gate = extra
    return x[...] + gate[...] * accs[0]


def _bias_kernel(rel_ref, bucket_ref, o_ref):
    h = pl.program_id(0)
    bucket = bucket_ref[...]
    acc = jnp.full(bucket.shape, NEG, F32)
    for b in range(N_BUCKETS):
        acc = jnp.where(bucket == b, rel_ref[b, h], acc)
    o_ref[...] = acc


def _t5_bucket(rel):
    nb = N_BUCKETS // 2
    max_exact = nb // 2
    ret = (rel > 0).astype(jnp.int32) * nb
    n = jnp.abs(rel)
    nf = jnp.maximum(n, 1).astype(F32)
    large = max_exact + (jnp.log(nf / max_exact) / np.float32(np.log(MAX_DISTANCE / max_exact))
                         * (nb - max_exact)).astype(jnp.int32)
    large = jnp.minimum(large, nb - 1)
    return ret + jnp.where(n < max_exact, n, large)


def _bias_table(rel_bias):
    q_idx = jnp.arange(BLOCK)[:, None]
    s_idx = jnp.arange(3 * BLOCK)[None, :]
    rel = s_idx - BLOCK - q_idx
    bucket = jnp.where(jnp.abs(rel) <= WINDOW, _t5_bucket(rel), -1).astype(jnp.int32)
    return pl.pallas_call(
        _bias_kernel,
        grid=(N_HEADS,),
        in_specs=[
            pl.BlockSpec(memory_space=pltpu.SMEM),
            pl.BlockSpec((BLOCK, 3 * BLOCK), lambda h: (0, 0)),
        ],
        out_specs=pl.BlockSpec((None, BLOCK, 3 * BLOCK), lambda h: (h, 0, 0)),
        out_shape=jax.ShapeDtypeStruct((N_HEADS, BLOCK, 3 * BLOCK), F32),
        compiler_params=_params(("arbitrary",)),
        name="rel_bias_table",
    )(rel_bias, bucket)


def _attn_kernel(sink_ref, q_ref, kc_ref, vc_ref, kp_ref, vp_ref, kn_ref, vn_ref, bias_ref, o_ref,
                 *, tiles_per_seq):
    t = pl.program_id(0) % tiles_per_seq
    nsub = ATT_ROWS // BLOCK
    gb = GROUP * BLOCK
    col = lax.broadcasted_iota(jnp.int32, (1, 3 * BLOCK), 1)
    head_row = lax.broadcasted_iota(jnp.int32, (gb, 1), 0)
    scale =np.float32(HEAD_DIM ** -0.5)

    def band(cur_ref, prev_ref, next_ref, j, cols):
        if j == 0:
            return jnp.concatenate([prev_ref[:, cols], cur_ref[0:2 * BLOCK, cols]], axis=0)
        if j == nsub - 1:
            return jnp.concatenate([cur_ref[(j - 1) * BLOCK:(j + 1) * BLOCK, cols], next_ref[:, cols]], axis=0)
        return cur_ref[(j - 1) * BLOCK:(j + 2) * BLOCK, cols]

    for j in range(nsub):
        rows = slice(j * BLOCK, (j + 1) * BLOCK)
        if j == 0:
            dead = (col < BLOCK) & (t == 0)
        elif j == nsub - 1:
            dead = (col >= 2 * BLOCK) & (t == tiles_per_seq - 1)
        else:
            dead = None
        for kv in range(N_KV_HEADS):
            cols = slice(kv * HEAD_DIM, (kv + 1) * HEAD_DIM)
            q = jnp.concatenate(
                [q_ref[rows, (kv * GROUP + g) * HEAD_DIM:(kv * GROUP + g + 1) * HEAD_DIM] for g in range(GROUP)],
                axis=0)
            kb = band(kc_ref, kp_ref, kn_ref, j, cols)
            vb = band(vc_ref, vp_ref, vn_ref, j, cols)
            s = lax.dot_general(q, kb, (((1,), (1,)), ((), ())), preferred_element_type=F32)
            s = s * scale + bias_ref[kv]
            if dead is not None:
                s = jnp.where(dead, NEG, s)
            sink = jnp.full((gb, 1), sink_ref[kv * GROUP], F32)
            for g in range(1, GROUP):
                sink = jnp.where(head_row >= g * BLOCK, sink_ref[kv * GROUP + g], sink)
            mx = jnp.maximum(jnp.max(s, axis=-1, keepdims=True), sink)
            p = jnp.exp(s - mx)
            denom = jnp.sum(p, axis=-1, keepdims=True) + jnp.exp(sink - mx)
            o = jnp.dot(p.astype(BF16), vb, preferred_element_type=F32) / denom
            for g in range(GROUP):
                h = kv * GROUP + g
                o_ref[rows, h * HEAD_DIM:(h + 1) * HEAD_DIM] = o[g * BLOCK:(g + 1) * BLOCK].astype(o_ref.dtype)


def _attention(z, bias_st, sink, seq):
    m = z.shape[0]
    tiles_per_seq = seq // ATT_ROWS
    sub = ATT_ROWS // BLOCK
    last_blk = m // BLOCK - 1
    kcol, vcol = OFF_K // D_KV, OFF_V // D_KV
    cur = lambda c: pl.BlockSpec((ATT_ROWS, D_KV), lambda i: (i, c))
    prev = lambda c: pl.BlockSpec((BLOCK, D_KV), lambda i: (jnp.maximum(i * sub - 1, 0), c))
    nxt = lambda c: pl.BlockSpec((BLOCK, D_KV), lambda i: (jnp.minimum(i * sub + sub, last_blk), c))
    return pl.pallas_call(
        functools.partial(_attn_kernel, tiles_per_seq=tiles_per_seq),
        grid=(m // ATT_ROWS,),
        in_specs=[
            pl.BlockSpec(memory_space=pltpu.SMEM),
            pl.BlockSpec((ATT_ROWS, D_ATT), lambda i: (i, OFF_Q // D_ATT)),
            cur(kcol), cur(vcol), prev(kcol), prev(vcol), nxt(kcol), nxt(vcol),
            pl.BlockSpec((N_KV_HEADS, GROUP * BLOCK, 3 * BLOCK), lambda i: (0, 0, 0)),
        ],
        out_specs=pl.BlockSpec((ATT_ROWS, D_ATT), lambda i: (i, 0)),
        out_shape=jax.ShapeDtypeStruct((m, D_ATT), BF16),
        compiler_params=_params(("arbitrary",)),
        name="window_gqa",
    )(sink, z, z, z, z, z, z, z, bias_st)


def _conv_kernel(u_ref, gb_ref, gc_ref, up_ref, gcp_ref, un_ref, gcn_ref, w_ref, o_ref, *, tiles_per_seq):
    t = pl.program_id(0) % tiles_per_seq
    p = u_ref[...].astype(F32) * gc_ref[...].astype(F32)
    first = (up_ref[...].astype(F32) * gcp_ref[...].astype(F32))[HALO - 1:HALO]
    last = (un_ref[...].astype(F32) * gcn_ref[...].astype(F32))[0:1]
    first = jnp.where(t == 0, 0.0, first)
    last = jnp.where(t == tiles_per_seq - 1, 0.0, last)
    row = lax.broadcasted_iota(jnp.int32, (CONV_ROWS, 1), 0)
    before = jnp.where(row == 0, first, pltpu.roll(p, 1, axis=0))
    after = jnp.where(row == CONV_ROWS - 1, last, pltpu.roll(p, CONV_ROWS - 1, axis=0))
    y = w_ref[0:1, :] * before + w_ref[1:2, :] * p + w_ref[2:3, :] * after
    o_ref[...] = (gb_ref[...].astype(F32) * y).astype(o_ref.dtype)


def _conv_gate(z, conv_w, seq):
    m = z.shape[0]
    tiles_per_seq = seq // CONV_ROWS
    per = CONV_ROWS // HALO
    last_blk = m // HALO - 1
    main = lambda off: pl.BlockSpec((CONV_ROWS, CONV_COLS), lambda i, c: (i, off // CONV_COLS + c))
    prev = lambda off: pl.BlockSpec((HALO, CONV_COLS),
                                    lambda i, c: (jnp.maximum(i * per - 1, 0), off // CONV_COLS + c))
    nxt = lambda off: pl.BlockSpec((HALO, CONV_COLS),
                                   lambda i, c: (jnp.minimum(i * per + per, last_blk), off // CONV_COLS + c))
    return pl.pallas_call(
        functools.partial(_conv_kernel, tiles_per_seq=tiles_per_seq),
        grid=(m // CONV_ROWS, D_CONV // CONV_COLS),
        in_specs=[main(OFF_U), main(OFF_GB), main(OFF_GC), prev(OFF_U), prev(OFF_GC), nxt(OFF_U), nxt(OFF_GC),
                  pl.BlockSpec((3, CONV_COLS), lambda i, c: (0, c))],
        out_specs=pl.BlockSpec((CONV_ROWS, CONV_COLS), lambda i, c: (i, c)),
        out_shape=jax.ShapeDtypeStruct((m, D_CONV), BF16),
        compiler_params=_params(("arbitrary", "arbitrary")),
        name="conv_gate",
    )(z, z, z, z, z, z, z, conv_w)


def kernel(x, c, rel_bias, norm1_g, norm2_g, w_ada, b_ada, w_in, b_merge, sink, conv_w,
           w_att_proj, w_conv_proj, w_out, w_mlp_in, w_mlp_out, final_g):
    bsz, seq, d = x.shape
    depth = w_ada.shape[0]
    m = bsz * seq
    tiles_per_batch = seq // MM_BM
    xf = x.reshape(m, d)

    c_pad = jnp.zeros((MOD_ROWS, d), F32).at[:bsz].set(c)
    mods = _modulation(c_pad, w_ada, b_ada)[:, :bsz].reshape(depth, bsz, N_MOD, 1, d)
    bias_st = _bias_table(rel_bias).reshape(N_KV_HEADS, GROUP * BLOCK, 3 * BLOCK)

    def mod_tile(arr):
        return (arr, (None, 1, MM_BN), lambda i, j: (i // tiles_per_batch, 0, j))

    for l in range(depth):
        shift1, scale1, gate1, shift2, scale2, gate2 = [mods[l, :, t] for t in range(N_MOD)]
        b_m = b_merge[l].reshape(1, 2 * d)

        h = _norm_mod(xf, norm1_g[l], scale1, shift1, seq)
        z = _matmul([(h, w_in[l].astype(BF16))], [], _ep_plain, D_IN, BF16, "in_proj")
        att = _attention(z, bias_st, sink[l], seq)
        bm = _conv_gate(z, conv_w[l], seq)
        merged = _matmul(
            [(att, w_att_proj[l].astype(BF16)), (bm, w_conv_proj[l].astype(BF16))],
            [(z, (MM_BM, MM_BN), lambda i, j: (i, OFF_GA // MM_BN + j)),
             (z, (MM_BM, MM_BN), lambda i, j: (i, OFF_GCV // MM_BN + j)),
             (b_m, (1, MM_BN), lambda i, j: (0, j)),
             (b_m, (1, MM_BN), lambda i, j: (0, d // MM_BN + j))],
            _ep_merge, d, BF16, "branch_merge")
        xf = _matmul([(merged, w_out[l].astype(BF16))],
                     [(xf, (MM_BM, MM_BN), lambda i, j: (i, j)), mod_tile(gate1)],
                     _ep_residual, d, F32, "out_proj")

        h2 = _norm_mod(xf, norm2_g[l], scale2, shift2, seq)
        a = _matmul([(h2, w_mlp_in[l].astype(BF16))], [], _ep_relu2, D_FF, BF16, "mlp_in")
        xf = _matmul([(a, w_mlp_out[l].astype(BF16))],
                     [(xf, (MM_BM, MM_BN), lambda i, j: (i, j)), mod_tile(gate2)],
                     _ep_residual, d, F32, "mlp_out", bk=MM_BK)

    return _final_norm(xf, final_g).reshape(bsz, seq, d)
```

```python
import functools

import jax
import jax.numpy as jnp
import numpy as np
from jax import lax
from jax.experimental import pallas as pl
from jax.experimental.pallas import tpu as pltpu

F32 = jnp.float32
BF16 = jnp.bfloat16

D_MODEL = 4096
N_HEADS = 16
HEAD_DIM = 128
N_KV_HEADS = 4
GROUP = N_HEADS // N_KV_HEADS
D_ATT = N_HEADS * HEAD_DIM
D_KV = N_KV_HEADS * HEAD_DIM
WINDOW = 128
BLOCK = 128
N_BUCKETS = 32
MAX_DISTANCE = 128
D_CONV = D_MODEL // 2
D_FF = 4 * D_MODEL
N_MOD = 6
EPS = 1e-6
NEG = -1e30

OFF_Q = 0
OFF_K = OFF_Q + D_ATT
OFF_V = OFF_K + D_KV
OFF_U = OFF_V + D_KV
OFF_GB = OFF_U + D_CONV
OFF_GC = OFF_GB + D_CONV
OFF_GA = OFF_GC + D_CONV
OFF_GCV = OFF_GA + D_MODEL
D_IN = OFF_GCV + D_MODEL

VMEM_LIMIT_BYTES = 56 * 1024 * 1024

MM_BM = 1024
MM_BN = 1024
MM_BK = 2048
OUT_BN = 512
NORM_ROWS = 256
ATT_ROWS = 512
CONV_ROWS = 512
CONV_COLS = 1024
HALO = 16
MOD_BN = 512
MOD_ROWS = 8


def _params(sem):
    return pltpu.CompilerParams(dimension_semantics=sem, vmem_limit_bytes=VMEM_LIMIT_BYTES)


def _mod_kernel(c_ref, w_ref, b_ref, o_ref):
    c = c_ref[...]
    c_act = c * jax.nn.sigmoid(c)
    acc = jnp.dot(c_act.astype(BF16), w_ref[...].astype(BF16), preferred_element_type=F32)
    o_ref[...] = acc + b_ref[...]


def _modulation(c_pad, w_ada, b_ada):
    depth, d, n = w_ada.shape
    return pl.pallas_call(
        _mod_kernel,
        grid=(depth, n // MOD_BN),
        in_specs=[
            pl.BlockSpec((MOD_ROWS, d), lambda l, j: (0, 0)),
            pl.BlockSpec((None, d, MOD_BN), lambda l, j: (l, 0, j)),
            pl.BlockSpec((None, 1, MOD_BN), lambda l, j: (l, 0, j)),
        ],
        out_specs=pl.BlockSpec((None, MOD_ROWS, MOD_BN), lambda l, j: (l, 0, j)),
        out_shape=jax.ShapeDtypeStruct((depth, MOD_ROWS, n), F32),
        compiler_params=_params(("arbitrary", "arbitrary")),
        name="adaln_mod",
    )(c_pad, w_ada, b_ada.reshape(depth, 1, n))


def _norm_mod_kernel(x_ref, g_ref, scale_ref, shift_ref, o_ref):
    x = x_ref[...]
    y = x * lax.rsqrt(jnp.mean(x * x, axis=-1, keepdims=True) + EPS)
    h = (y * g_ref[...]) * (1.0 + scale_ref[...]) + shift_ref[...]
    o_ref[...] = h.astype(o_ref.dtype)


def _norm_kernel(x_ref, g_ref, o_ref):
    x = x_ref[...]
    y = x * lax.rsqrt(jnp.mean(x * x, axis=-1, keepdims=True) + EPS)
    o_ref[...] = (y * g_ref[...]).astype(o_ref.dtype)


def _norm_mod(x, g, scale, shift, seq):
    m, d = x.shape
    per_batch = seq // NORM_ROWS
    row = pl.BlockSpec((NORM_ROWS, d), lambda i: (i, 0))
    mod = pl.BlockSpec((None, 1, d), lambda i: (i // per_batch, 0, 0))
    return pl.pallas_call(
        _norm_mod_kernel,
        grid=(m // NORM_ROWS,),
        in_specs=[row, pl.BlockSpec((1, d), lambda i: (0, 0)), mod, mod],
        out_specs=row,
        out_shape=jax.ShapeDtypeStruct((m, d), BF16),
        compiler_params=_params(("arbitrary",)),
        name="norm_mod",
    )(x, g.reshape(1, d), scale, shift)


def _final_norm(x, g):
    m, d = x.shape
    row = pl.BlockSpec((NORM_ROWS, d), lambda i: (i, 0))
    return pl.pallas_call(
        _norm_kernel,
        grid=(m // NORM_ROWS,),
        in_specs=[row, pl.BlockSpec((1, d), lambda i: (0, 0))],
        out_specs=row,
        out_shape=jax.ShapeDtypeStruct((m, d), F32),
        compiler_params=_params(("arbitrary",)),
        name="final_norm",
    )(x, g.reshape(1, d))


def _mm_resident_kernel(*refs, npairs, nextra, epilogue):
    a_refs = refs[:npairs]
    wchunk_refs = refs[npairs:2 * npairs]
    extra = refs[2 * npairs:2 * npairs + nextra]
    o_ref = refs[2 * npairs + nextra]
    wbuf_refs = refs[2 * npairs + nextra + 1:]
    n = pl.program_id(0)
    i = pl.program_id(1)

    def convert_chunk():
        for wchunk, wbuf in zip(wchunk_refs, wbuf_refs):
            rows = wchunk.shape[0]
            wbuf[n % 2, pl.ds(pl.multiple_of(i * rows, rows), rows), :] = wchunk[...].astype(BF16)

    @pl.when(n == 0)
    def _():
        convert_chunk()

    @pl.when(n > 0)
    def _():
        convert_chunk()
        slot = (n + 1) % 2
        accs = [jnp.dot(a[...], wbuf[slot], preferred_element_type=F32) for a, wbuf in zip(a_refs, wbuf_refs)]
        o_ref[...] = epilogue(accs, extra).astype(o_ref.dtype)


def _matmul_resident(pairs, layer, extras, epilogue, n_out, out_dtype, name, bn=MM_BN):
    m = pairs[0][0].shape[0]
    mt, nt = m // MM_BM, n_out // bn

    def at_tile(f):
        return lambda n, i: f(jnp.where(n > 0, i, 0), jnp.maximum(n - 1, 0))

    a_specs = [pl.BlockSpec((MM_BM, a.shape[1]), at_tile(lambda i, j: (i, 0))) for a, _ in pairs]
    w_specs = [pl.BlockSpec((None, w.shape[1] // mt, bn), lambda n, i: (layer, i, jnp.minimum(n, nt - 1)))
               for _, w in pairs]
    ex_specs = [pl.BlockSpec(bs, at_tile(f)) for _, bs, f in extras]
    body = functools.partial(_mm_resident_kernel, npairs=len(pairs), nextra=len(extras), epilogue=epilogue)
    return pl.pallas_call(
        body,
        grid=(nt + 1, mt),
        in_specs=a_specs + w_specs + ex_specs,
        out_specs=pl.BlockSpec((MM_BM, bn), at_tile(lambda i, j: (i, j))),
        out_shape=jax.ShapeDtypeStruct((m, n_out), out_dtype),
        scratch_shapes=[pltpu.VMEM((2, w.shape[1], bn), BF16) for _, w in pairs],
        compiler_params=_params(("arbitrary", "arbitrary")),
        name=name,
    )(*[a for a, _ in pairs], *[w for _, w in pairs], *[e for e, _, _ in extras])


def _mm_ksplit_kernel(a_ref, w_ref, x_ref, gate_ref, o_ref, acc_ref, *, nk):
    k = pl.program_id(2)

    def part():
        return jnp.dot(a_ref[...], w_ref[...].astype(BF16), preferred_element_type=F32)

    @pl.when(k == 0)
    def _():
        acc_ref[...] = part()

    @pl.when((k > 0) & (k < nk - 1))
    def _():
        acc_ref[...] += part()

    @pl.when(k == nk - 1)
    def _():
        o_ref[...] = x_ref[...] + gate_ref[...] * (acc_ref[...] + part())


def _matmul_ksplit_residual(a, w, layer, x, gate, gate_map, name):
    m, kdim = a.shape
    n_out = w.shape[2]
    nk = kdim // MM_BK
    tile = pl.BlockSpec((MM_BM, MM_BN), lambda i, j, k: (i, j))
    return pl.pallas_call(
        functools.partial(_mm_ksplit_kernel, nk=nk),
        grid=(m // MM_BM, n_out // MM_BN, nk),
        in_specs=[
            pl.BlockSpec((MM_BM, MM_BK), lambda i, j, k: (i, k)),
            pl.BlockSpec((None, MM_BK, MM_BN), lambda i, j, k: (layer, k, j)),
            tile,
            pl.BlockSpec((None, 1, MM_BN), lambda i, j, k: gate_map(i, j)),
        ],
        out_specs=tile,
        out_shape=jax.ShapeDtypeStruct((m, n_out), F32),
        scratch_shapes=[pltpu.VMEM((MM_BM, MM_BN), F32)],
        compiler_params=_params(("arbitrary", "arbitrary", "arbitrary")),
        name=name,
    )(a, w, x, gate)


def _ep_plain(accs, extra):
    return accs[0]


def _ep_relu2(accs, extra):
    return jnp.square(jnp.maximum(accs[0], 0.0))


def _ep_merge(accs, extra):
    glog_att, glog_conv, b_att, b_conv = extra
    g_att = jax.nn.sigmoid(glog_att[...].astype(F32) + b_att[...])
    g_conv = jax.nn.sigmoid(glog_conv[...].astype(F32) + b_conv[...])
    return g_att * accs[0] + g_conv * accs[1]


def _ep_residual(accs, extra):
    x, gate = extra
    return x[...] + gate[...] * accs[0]


def _bias_kernel(rel_ref, bucket_ref, o_ref):
    h = pl.program_id(0)
    bucket = bucket_ref[...]
    acc = jnp.full(bucket.shape, NEG, F32)
    for b in range(N_BUCKETS):
        acc = jnp.where(bucket == b, rel_ref[b, h], acc)
    o_ref[...] = acc


def _t5_bucket(rel):
    nb = N_BUCKETS // 2
    max_exact = nb // 2
    ret = (rel > 0).astype(jnp.int32) * nb
    n = jnp.abs(rel)
    nf = jnp.maximum(n, 1).astype(F32)
    large = max_exact + (jnp.log(nf / max_exact) / np.float32(np.log(MAX_DISTANCE / max_exact))
                         * (nb - max_exact)).astype(jnp.int32)
    large = jnp.minimum(large, nb - 1)
    return ret + jnp.where(n < max_exact, n, large)


def _bias_table(rel_bias):
    q_idx = jnp.arange(BLOCK)[:, None]
    s_idx = jnp.arange(3 * BLOCK)[None, :]
    rel = s_idx - BLOCK - q_idx
    bucket = jnp.where(jnp.abs(rel) <= WINDOW, _t5_bucket(rel), -1).astype(jnp.int32)
    return pl.pallas_call(
        _bias_kernel,
        grid=(N_HEADS,),
        in_specs=[
            pl.BlockSpec(memory_space=pltpu.SMEM),
            pl.BlockSpec((BLOCK, 3 * BLOCK), lambda h: (0, 0)),
        ],
        out_specs=pl.BlockSpec((None, BLOCK, 3 * BLOCK), lambda h: (h, 0, 0)),
        out_shape=jax.ShapeDtypeStruct((N_HEADS, BLOCK, 3 * BLOCK), F32),
        compiler_params=_params(("arbitrary",)),
        name="rel_bias_table",
    )(rel_bias, bucket)


def _attn_kernel(sink_ref, q_ref, kc_ref, vc_ref, kp_ref, vp_ref, kn_ref, vn_ref, bias_ref, o_ref,
                 *, tiles_per_seq):
    t = pl.program_id(0) % tiles_per_seq
    nsub = ATT_ROWS // BLOCK
    gb = GROUP * BLOCK
    col = lax.broadcasted_iota(jnp.int32, (1, 3 * BLOCK), 1)
    head_row = lax.broadcasted_iota(jnp.int32, (gb, 1), 0)
    scale =np.float32(HEAD_DIM ** -0.5)

    def band(cur_ref, prev_ref, next_ref, j, cols):
        if j == 0:
            return jnp.concatenate([prev_ref[:, cols], cur_ref[0:2 * BLOCK, cols]], axis=0)
        if j == nsub - 1:
            return jnp.concatenate([cur_ref[(j - 1) * BLOCK:(j + 1) * BLOCK, cols], next_ref[:, cols]], axis=0)
        return cur_ref[(j - 1) * BLOCK:(j + 2) * BLOCK, cols]

    for j in range(nsub):
        rows = slice(j * BLOCK, (j + 1) * BLOCK)
        if j == 0:
            dead = (col < BLOCK) & (t == 0)
        elif j == nsub - 1:
            dead = (col >= 2 * BLOCK) & (t == tiles_per_seq - 1)
        else:
            dead = None
        for kv in range(N_KV_HEADS):
            cols = slice(kv * HEAD_DIM, (kv + 1) * HEAD_DIM)
            q = jnp.concatenate(
                [q_ref[rows, (kv * GROUP + g) * HEAD_DIM:(kv * GROUP + g + 1) * HEAD_DIM] for g in range(GROUP)],
                axis=0)
            kb = band(kc_ref, kp_ref, kn_ref, j, cols)
            vb = band(vc_ref, vp_ref, vn_ref, j, cols)
            s = lax.dot_general(q, kb, (((1,), (1,)), ((), ())), preferred_element_type=F32)
            s = s * scale + bias_ref[kv]
            if dead is not None:
                s = jnp.where(dead, NEG, s)
            sink = jnp.full((gb, 1), sink_ref[kv * GROUP], F32)
            for g in range(1, GROUP):
                sink = jnp.where(head_row >= g * BLOCK, sink_ref[kv * GROUP + g], sink)
            mx = jnp.maximum(jnp.max(s, axis=-1, keepdims=True), sink)
            p = jnp.exp(s - mx)
            denom = jnp.sum(p, axis=-1, keepdims=True) + jnp.exp(sink - mx)
            o = jnp.dot(p.astype(BF16), vb, preferred_element_type=F32) / denom
            for g in range(GROUP):
                h = kv * GROUP + g
                o_ref[rows, h * HEAD_DIM:(h + 1) * HEAD_DIM] = o[g * BLOCK:(g + 1) * BLOCK].astype(o_ref.dtype)


def _attention(z, bias_st, sink, seq):
    m = z.shape[0]
    tiles_per_seq = seq // ATT_ROWS
    sub = ATT_ROWS // BLOCK
    last_blk = m // BLOCK - 1
    kcol, vcol = OFF_K // D_KV, OFF_V // D_KV
    cur = lambda c: pl.BlockSpec((ATT_ROWS, D_KV), lambda i: (i, c))
    prev = lambda c: pl.BlockSpec((BLOCK, D_KV), lambda i: (jnp.maximum(i * sub - 1, 0), c))
    nxt = lambda c: pl.BlockSpec((BLOCK, D_KV), lambda i: (jnp.minimum(i * sub + sub, last_blk), c))
    return pl.pallas_call(
        functools.partial(_attn_kernel, tiles_per_seq=tiles_per_seq),
        grid=(m // ATT_ROWS,),
        in_specs=[
            pl.BlockSpec(memory_space=pltpu.SMEM),
            pl.BlockSpec((ATT_ROWS, D_ATT), lambda i: (i, OFF_Q // D_ATT)),
            cur(kcol), cur(vcol), prev(kcol), prev(vcol), nxt(kcol), nxt(vcol),
            pl.BlockSpec((N_KV_HEADS, GROUP * BLOCK, 3 * BLOCK), lambda i: (0, 0, 0)),
        ],
        out_specs=pl.BlockSpec((ATT_ROWS, D_ATT), lambda i: (i, 0)),
        out_shape=jax.ShapeDtypeStruct((m, D_ATT), BF16),
        compiler_params=_params(("arbitrary",)),
        name="window_gqa",
    )(sink, z, z, z, z, z, z, z, bias_st)


def _conv_kernel(u_ref, gb_ref, gc_ref, up_ref, gcp_ref, un_ref, gcn_ref, w_ref, o_ref, *, tiles_per_seq):
    t = pl.program_id(0) % tiles_per_seq
    p = u_ref[...].astype(F32) * gc_ref[...].astype(F32)
    first = (up_ref[...].astype(F32) * gcp_ref[...].astype(F32))[HALO - 1:HALO]
    last = (un_ref[...].astype(F32) * gcn_ref[...].astype(F32))[0:1]
    first = jnp.where(t == 0, 0.0, first)
    last = jnp.where(t == tiles_per_seq - 1, 0.0, last)
    row = lax.broadcasted_iota(jnp.int32, (CONV_ROWS, 1), 0)
    before = jnp.where(row == 0, first, pltpu.roll(p, 1, axis=0))
    after = jnp.where(row == CONV_ROWS - 1, last, pltpu.roll(p, CONV_ROWS - 1, axis=0))
    y = w_ref[0:1, :] * before + w_ref[1:2, :] * p + w_ref[2:3, :] * after
    o_ref[...] = (gb_ref[...].astype(F32) * y).astype(o_ref.dtype)


def _conv_gate(z, conv_w, seq):
    m = z.shape[0]
    tiles_per_seq = seq // CONV_ROWS
    per = CONV_ROWS // HALO
    last_blk = m // HALO - 1
    main = lambda off: pl.BlockSpec((CONV_ROWS, CONV_COLS), lambda i, c: (i, off // CONV_COLS + c))
    prev = lambda off: pl.BlockSpec((HALO, CONV_COLS),
                                    lambda i, c: (jnp.maximum(i * per - 1, 0), off // CONV_COLS + c))
    nxt = lambda off: pl.BlockSpec((HALO, CONV_COLS),
                                   lambda i, c: (jnp.minimum(i * per + per, last_blk), off // CONV_COLS + c))
    return pl.pallas_call(
        functools.partial(_conv_kernel, tiles_per_seq=tiles_per_seq),
        grid=(m // CONV_ROWS, D_CONV // CONV_COLS),
        in_specs=[main(OFF_U), main(OFF_GB), main(OFF_GC), prev(OFF_U), prev(OFF_GC), nxt(OFF_U), nxt(OFF_GC),
                  pl.BlockSpec((3, CONV_COLS), lambda i, c: (0, c))],
        out_specs=pl.BlockSpec((CONV_ROWS, CONV_COLS), lambda i, c: (i, c)),
        out_shape=jax.ShapeDtypeStruct((m, D_CONV), BF16),
        compiler_params=_params(("arbitrary", "arbitrary")),
        name="conv_gate",
    )(z, z, z, z, z, z, z, conv_w)


def kernel(x, c, rel_bias, norm1_g, norm2_g, w_ada, b_ada, w_in, b_merge, sink, conv_w,
           w_att_proj, w_conv_proj, w_out, w_mlp_in, w_mlp_out, final_g):
    bsz, seq, d = x.shape
    depth = w_ada.shape[0]
    m = bsz * seq
    tiles_per_batch = seq // MM_BM
    xf = x.reshape(m, d)

    c_pad = jnp.zeros((MOD_ROWS, d), F32).at[:bsz].set(c)
    mods = _modulation(c_pad, w_ada, b_ada)[:, :bsz].reshape(depth, bsz, N_MOD, 1, d)
    bias_st = _bias_table(rel_bias).reshape(N_KV_HEADS, GROUP * BLOCK, 3 * BLOCK)

    def gate_map(i, j):
        return (i // tiles_per_batch, 0, j)

    for l in range(depth):
        shift1, scale1, gate1, shift2, scale2, gate2 = [mods[l, :, t] for t in range(N_MOD)]
        b_m = b_merge[l].reshape(1, 2 * d)

        h = _norm_mod(xf, norm1_g[l], scale1, shift1, seq)
        z = _matmul_resident([(h, w_in)], l, [], _ep_plain, D_IN, BF16, "in_proj")
        att = _attention(z, bias_st, sink[l], seq)
        bm = _conv_gate(z, conv_w[l], seq)
        merged = _matmul_resident(
            [(att, w_att_proj), (bm, w_conv_proj)], l,
            [(z, (MM_BM, MM_BN), lambda i, j: (i, OFF_GA // MM_BN + j)),
             (z, (MM_BM, MM_BN), lambda i, j: (i, OFF_GCV // MM_BN + j)),
             (b_m, (1, MM_BN), lambda i, j: (0, j)),
             (b_m, (1, MM_BN), lambda i, j: (0, d // MM_BN + j))],
            _ep_merge, d, BF16, "branch_merge")
        xf = _matmul_resident([(merged, w_out)], l,
                              [(xf, (MM_BM, OUT_BN), lambda i, j: (i, j)), (gate1, (None, 1, OUT_BN), gate_map)],
                              _ep_residual, d, F32, "out_proj", bn=OUT_BN)

        h2 = _norm_mod(xf, norm2_g[l], scale2, shift2, seq)
        a = _matmul_resident([(h2, w_mlp_in)], l, [], _ep_relu2, D_FF, BF16, "mlp_in")
        xf = _matmul_ksplit_residual(a, w_mlp_out, l, xf, gate2, gate_map, "mlp_out")

    return _final_norm(xf, final_g).reshape(bsz, seq, d)
```

```python
import functools

import jax
import jax.numpy as jnp
import numpy as np
from jax import lax
from jax.experimental import pallas as pl
from jax.experimental.pallas import tpu as pltpu

F32 = jnp.float32
BF16 = jnp.bfloat16

D_MODEL = 4096
N_HEADS = 16
HEAD_DIM = 128
N_KV_HEADS = 4
GROUP = N_HEADS // N_KV_HEADS
D_ATT = N_HEADS * HEAD_DIM
D_KV = N_KV_HEADS * HEAD_DIM
WINDOW = 128
BLOCK = 128
N_BUCKETS = 32
MAX_DISTANCE = 128
D_CONV = D_MODEL // 2
D_FF = 4 * D_MODEL
N_MOD = 6
EPS = 1e-6
NEG = -1e30

OFF_Q = 0
OFF_K = OFF_Q + D_ATT
OFF_V = OFF_K + D_KV
OFF_U = OFF_V + D_KV
OFF_GB = OFF_U + D_CONV
OFF_GC = OFF_GB + D_CONV
OFF_GA = OFF_GC + D_CONV
OFF_GCV = OFF_GA + D_MODEL
D_IN = OFF_GCV + D_MODEL

VMEM_LIMIT_BYTES = 56 * 1024 * 1024

MM_BM = 1024
MM_BN = 1024
MM_BK = 2048
OUT_BN = 512
LANES = 128
NORM_ROWS = 512
NORM_HALF = 256
ATT_ROWS = 512
CONV_ROWS = 512
CONV_COLS = 1024
HALO = 16
MOD_BN = 512
MOD_ROWS = 8


def _params(sem):
    return pltpu.CompilerParams(dimension_semantics=sem, vmem_limit_bytes=VMEM_LIMIT_BYTES)


def _mod_kernel(c_ref, w_ref, b_ref, o_ref):
    c = c_ref[...]
    c_act = c * jax.nn.sigmoid(c)
    acc = jnp.dot(c_act.astype(BF16), w_ref[...].astype(BF16), preferred_element_type=F32)
    o_ref[...] = acc + b_ref[...]


def _modulation(c_pad, w_ada, b_ada):
    depth, d, n = w_ada.shape
    return pl.pallas_call(
        _mod_kernel,
        grid=(depth, n // MOD_BN),
        in_specs=[
            pl.BlockSpec((MOD_ROWS, d), lambda l, j: (0, 0)),
            pl.BlockSpec((None, d, MOD_BN), lambda l, j: (l, 0, j)),
            pl.BlockSpec((None, 1, MOD_BN), lambda l, j: (l, 0, j)),
        ],
        out_specs=pl.BlockSpec((None, MOD_ROWS, MOD_BN), lambda l, j: (l, 0, j)),
        out_shape=jax.ShapeDtypeStruct((depth, MOD_ROWS, n), F32),
        compiler_params=_params(("arbitrary", "arbitrary")),
        name="adaln_mod",
    )(c_pad, w_ada, b_ada.reshape(depth, 1, n))


def _inv_rms(x_ref, rstd_ref):
    d = x_ref.shape[1]
    for r in range(NORM_ROWS // NORM_HALF):
        rows = slice(r * NORM_HALF, (r + 1) * NORM_HALF)
        acc = jnp.zeros((NORM_HALF, LANES), F32)
        for c in range(d // LANES):
            x = x_ref[rows, c * LANES:(c + 1) * LANES]
            acc = acc + x * x
        mean_sq = jnp.sum(acc, axis=-1, keepdims=True) / d
        rstd_ref[rows, :] = jnp.broadcast_to(lax.rsqrt(mean_sq + EPS), (NORM_HALF, LANES))


def _norm_mod_kernel(x_ref, g_ref, scale_ref, shift_ref, o_ref, rstd_ref):
    _inv_rms(x_ref, rstd_ref)
    for c in range(x_ref.shape[1] // LANES):
        cols = slice(c * LANES, (c + 1) * LANES)
        y = x_ref[:, cols] * rstd_ref[...]
        h = (y * g_ref[:, cols]) * (1.0 + scale_ref[:, cols]) + shift_ref[:, cols]
        o_ref[:, cols] = h.astype(o_ref.dtype)


def _norm_kernel(x_ref, g_ref, o_ref, rstd_ref):
    _inv_rms(x_ref, rstd_ref)
    for c in range(x_ref.shape[1] // LANES):
        cols = slice(c * LANES, (c + 1) * LANES)
        y = x_ref[:, cols] * rstd_ref[...]
        o_ref[:, cols] = (y * g_ref[:, cols]).astype(o_ref.dtype)


def _norm_mod(x, g, scale, shift, seq):
    m, d = x.shape
    per_batch = seq // NORM_ROWS
    row = pl.BlockSpec((NORM_ROWS, d), lambda i: (i, 0))
    mod = pl.BlockSpec((None, 1, d), lambda i: (i // per_batch, 0, 0))
    return pl.pallas_call(
        _norm_mod_kernel,
        grid=(m // NORM_ROWS,),
        in_specs=[row, pl.BlockSpec((1, d), lambda i: (0, 0)), mod, mod],
        out_specs=row,
        out_shape=jax.ShapeDtypeStruct((m, d), BF16),
        scratch_shapes=[pltpu.VMEM((NORM_ROWS, LANES), F32)],
        compiler_params=_params(("arbitrary",)),
        name="norm_mod",
    )(x, g.reshape(1, d), scale, shift)


def _final_norm(x, g):
    m, d = x.shape
    row = pl.BlockSpec((NORM_ROWS, d), lambda i: (i, 0))
    return pl.pallas_call(
        _norm_kernel,
        grid=(m // NORM_ROWS,),
        in_specs=[row, pl.BlockSpec((1, d), lambda i: (0, 0))],
        out_specs=row,
        out_shape=jax.ShapeDtypeStruct((m, d), F32),
        scratch_shapes=[pltpu.VMEM((NORM_ROWS, LANES), F32)],
        compiler_params=_params(("arbitrary",)),
        name="final_norm",
    )(x, g.reshape(1, d))


def _mm_resident_kernel(*refs, npairs, nextra, nside, epilogue):
    a_refs = refs[:npairs]
    wchunk_refs = refs[npairs:2 * npairs]
    extra = refs[2 * npairs:2 * npairs + nextra]
    side_in = refs[2 * npairs + nextra:2 * npairs + nextra + nside]
    o_ref = refs[2 * npairs + nextra + nside]
    side_out = refs[2 * npairs + nextra + nside + 1:2 * npairs + nextra + 2 * nside + 1]
    wbuf_refs = refs[2 * npairs + nextra + 2 * nside + 1:]
    n = pl.program_id(0)
    i = pl.program_id(1)

    def convert_chunk():
        for wchunk, wbuf in zip(wchunk_refs, wbuf_refs):
            rows = wchunk.shape[0]
            wbuf[n % 2, pl.ds(pl.multiple_of(i * rows, rows), rows), :] = wchunk[...].astype(BF16)

    @pl.when(n == 0)
    def _():
        convert_chunk()

    @pl.when(n > 0)
    def _():
        convert_chunk()
        for src, dst in zip(side_in, side_out):
            dst[...] = src[...].astype(dst.dtype)
        slot = (n + 1) % 2
        accs = [jnp.dot(a[...], wbuf[slot], preferred_element_type=F32) for a, wbuf in zip(a_refs, wbuf_refs)]
        o_ref[...] = epilogue(accs, extra).astype(o_ref.dtype)


def _matmul_resident(pairs, layer, extras, epilogue, n_out, out_dtype, name, bn=MM_BN, convert=None):
    m = pairs[0][0].shape[0]
    mt, nt = m // MM_BM, n_out // bn

    def at_tile(f):
        return lambda n, i: f(jnp.where(n > 0, i, 0), jnp.maximum(n - 1, 0))

    a_specs = [pl.BlockSpec((MM_BM, a.shape[1]), at_tile(lambda i, j: (i, 0))) for a, _ in pairs]
    w_specs = [pl.BlockSpec((None, w.shape[1] // mt, bn), lambda n, i: (layer, i, jnp.minimum(n, nt - 1)))
               for _, w in pairs]
    ex_specs = [pl.BlockSpec(bs, at_tile(f)) for _, bs, f in extras]
    out_specs = [pl.BlockSpec((MM_BM, bn), at_tile(lambda i, j: (i, j)))]
    out_shape = [jax.ShapeDtypeStruct((m, n_out), out_dtype)]
    side_specs, side_args = [], []
    if convert is not None:
        _, r, c = convert.shape
        rows = r // (nt * mt)
        side_specs = [pl.BlockSpec((None, rows, c), at_tile(lambda i, j: (layer, j * mt + i, 0)))]
        side_args = [convert]
        out_specs.append(pl.BlockSpec((rows, c), at_tile(lambda i, j: (j * mt + i, 0))))
        out_shape.append(jax.ShapeDtypeStruct((r, c), BF16))
    body = functools.partial(_mm_resident_kernel, npairs=len(pairs), nextra=len(extras), nside=len(side_args),
                             epilogue=epilogue)
    outs = pl.pallas_call(
        body,
        grid=(nt + 1, mt),
        in_specs=a_specs + w_specs + ex_specs + side_specs,
        out_specs=out_specs,
        out_shape=out_shape,
        scratch_shapes=[pltpu.VMEM((2, w.shape[1], bn), BF16) for _, w in pairs],
        compiler_params=_params(("arbitrary", "arbitrary")),
        name=name,
    )(*[a for a, _ in pairs], *[w for _, w in pairs], *[e for e, _, _ in extras], *side_args)
    return outs[0] if convert is None else outs


def _mm_ksplit_kernel(a_ref, w_ref, x_ref, gate_ref, o_ref, acc_ref, *, nk):
    k = pl.program_id(2)

    def part():
        return jnp.dot(a_ref[...], w_ref[...], preferred_element_type=F32)

    @pl.when(k == 0)
    def _():
        acc_ref[...] = part()

    @pl.when((k > 0) & (k < nk - 1))
    def _():
        acc_ref[...] += part()

    @pl.when(k == nk - 1)
    def _():
        o_ref[...] = x_ref[...] + gate_ref[...] * (acc_ref[...] + part())


def _matmul_ksplit_residual(a, w, x, gate, gate_map, name):
    m, kdim = a.shape
    n_out = w.shape[1]
    nk = kdim // MM_BK
    tile = pl.BlockSpec((MM_BM, MM_BN), lambda i, j, k: (i, j))
    return pl.pallas_call(
        functools.partial(_mm_ksplit_kernel, nk=nk),
        grid=(m // MM_BM, n_out // MM_BN, nk),
        in_specs=[
            pl.BlockSpec((MM_BM, MM_BK), lambda i, j, k: (i, k)),
            pl.BlockSpec((MM_BK, MM_BN), lambda i, j, k: (k, j)),
            tile,
            pl.BlockSpec((None, 1, MM_BN), lambda i, j, k: gate_map(i, j)),
        ],
        out_specs=tile,
        out_shape=jax.ShapeDtypeStruct((m, n_out), F32),
        scratch_shapes=[pltpu.VMEM((MM_BM, MM_BN), F32)],
        compiler_params=_params(("arbitrary", "arbitrary", "arbitrary")),
        name=name,
    )(a, w, x, gate)


def _ep_plain(accs, extra):
    return accs[0]


def _ep_relu2(accs, extra):
    return jnp.square(jnp.maximum(accs[0], 0.0))


def _ep_merge(accs, extra):
    glog_att, glog_conv, b_att, b_conv = extra
    g_att = jax.nn.sigmoid(glog_att[...].astype(F32) + b_att[...])
    g_conv = jax.nn.sigmoid(glog_conv[...].astype(F32) + b_conv[...])
    return g_att * accs[0] + g_conv * accs[1]


def _ep_residual(accs, extra):
    x, gate = extra
    return x[...] + gate[...] * accs[0]


def _bias_kernel(rel_ref, bucket_ref, o_ref):
    h = pl.program_id(0)
    bucket = bucket_ref[...]
    acc = jnp.full(bucket.shape, NEG, F32)
    for b in range(N_BUCKETS):
        acc = jnp.where(bucket == b, rel_ref[b, h], acc)
    o_ref[...] = acc


def _t5_bucket(rel):
    nb = N_BUCKETS // 2
    max_exact = nb // 2
    ret = (rel > 0).astype(jnp.int32) * nb
    n = jnp.abs(rel)
    nf = jnp.maximum(n, 1).astype(F32)
    large = max_exact + (jnp.log(nf / max_exact) / np.float32(np.log(MAX_DISTANCE / max_exact))
                         * (nb - max_exact)).astype(jnp.int32)
    large = jnp.minimum(large, nb - 1)
    return ret + jnp.where(n < max_exact, n, large)


def _bias_table(rel_bias):
    q_idx = jnp.arange(BLOCK)[:, None]
    s_idx = jnp.arange(3 * BLOCK)[None, :]
    rel = s_idx - BLOCK - q_idx
    bucket = jnp.where(jnp.abs(rel) <= WINDOW, _t5_bucket(rel), -1).astype(jnp.int32)
    return pl.pallas_call(
        _bias_kernel,
        grid=(N_HEADS,),
        in_specs=[
            pl.BlockSpec(memory_space=pltpu.SMEM),
            pl.BlockSpec((BLOCK, 3 * BLOCK), lambda h: (0, 0)),
        ],
        out_specs=pl.BlockSpec((None, BLOCK, 3 * BLOCK), lambda h: (h, 0, 0)),
        out_shape=jax.ShapeDtypeStruct((N_HEADS, BLOCK, 3 * BLOCK), F32),
        compiler_params=_params(("arbitrary",)),
        name="rel_bias_table",
    )(rel_bias, bucket)


def _attn_kernel(sink_ref, q_ref, kc_ref, vc_ref, kp_ref, vp_ref, kn_ref, vn_ref, bias_ref, o_ref,
                 *, tiles_per_seq):
    t = pl.program_id(0) % tiles_per_seq
    nsub = ATT_ROWS // BLOCK
    gb = GROUP * BLOCK
    col = lax.broadcasted_iota(jnp.int32, (1, 3 * BLOCK), 1)
    head_row = lax.broadcasted_iota(jnp.int32, (gb, 1), 0)
    scale =np.float32(HEAD_DIM ** -0.5)

    def band(cur_ref, prev_ref, next_ref, j, cols):
        if j == 0:
            return jnp.concatenate([prev_ref[:, cols], cur_ref[0:2 * BLOCK, cols]], axis=0)
        if j == nsub - 1:
            return jnp.concatenate([cur_ref[(j - 1) * BLOCK:(j + 1) * BLOCK, cols], next_ref[:, cols]], axis=0)
        return cur_ref[(j - 1) * BLOCK:(j + 2) * BLOCK, cols]

    for j in range(nsub):
        rows = slice(j * BLOCK, (j + 1) * BLOCK)
        if j == 0:
            dead = (col < BLOCK) & (t == 0)
        elif j == nsub - 1:
            dead = (col >= 2 * BLOCK) & (t == tiles_per_seq - 1)
        else:
            dead = None
        for kv in range(N_KV_HEADS):
            cols = slice(kv * HEAD_DIM, (kv + 1) * HEAD_DIM)
            q = jnp.concatenate(
                [q_ref[rows, (kv * GROUP + g) * HEAD_DIM:(kv * GROUP + g + 1) * HEAD_DIM] for g in range(GROUP)],
                axis=0)
            kb = band(kc_ref, kp_ref, kn_ref, j, cols)
            vb = band(vc_ref, vp_ref, vn_ref, j, cols)
            s = lax.dot_general(q, kb, (((1,), (1,)), ((), ())), preferred_element_type=F32)
            s = s * scale + bias_ref[kv]
            if dead is not None:
                s = jnp.where(dead, NEG, s)
            sink = jnp.full((gb, 1), sink_ref[kv * GROUP], F32)
            for g in range(1, GROUP):
                sink = jnp.where(head_row >= g * BLOCK, sink_ref[kv * GROUP + g], sink)
            mx = jnp.maximum(jnp.max(s, axis=-1, keepdims=True), sink)
            p = jnp.exp(s - mx)
            denom = jnp.sum(p, axis=-1, keepdims=True) + jnp.exp(sink - mx)
            o = jnp.dot(p.astype(BF16), vb, preferred_element_type=F32) / denom
            for g in range(GROUP):
                h = kv * GROUP + g
                o_ref[rows, h * HEAD_DIM:(h + 1) * HEAD_DIM] = o[g * BLOCK:(g + 1) * BLOCK].astype(o_ref.dtype)


def _attention(z, bias_st, sink, seq):
    m = z.shape[0]
    tiles_per_seq = seq // ATT_ROWS
    sub = ATT_ROWS // BLOCK
    last_blk = m // BLOCK - 1
    kcol, vcol = OFF_K // D_KV, OFF_V // D_KV
    cur = lambda c: pl.BlockSpec((ATT_ROWS, D_KV), lambda i: (i, c))
    prev = lambda c: pl.BlockSpec((BLOCK, D_KV), lambda i: (jnp.maximum(i * sub - 1, 0), c))
    nxt = lambda c: pl.BlockSpec((BLOCK, D_KV), lambda i: (jnp.minimum(i * sub + sub, last_blk), c))
    return pl.pallas_call(
        functools.partial(_attn_kernel, tiles_per_seq=tiles_per_seq),
        grid=(m // ATT_ROWS,),
        in_specs=[
            pl.BlockSpec(memory_space=pltpu.SMEM),
            pl.BlockSpec((ATT_ROWS, D_ATT), lambda i: (i, OFF_Q // D_ATT)),
            cur(kcol), cur(vcol), prev(kcol), prev(vcol), nxt(kcol), nxt(vcol),
            pl.BlockSpec((N_KV_HEADS, GROUP * BLOCK, 3 * BLOCK), lambda i: (0, 0, 0)),
        ],
        out_specs=pl.BlockSpec((ATT_ROWS, D_ATT), lambda i: (i, 0)),
        out_shape=jax.ShapeDtypeStruct((m, D_ATT), BF16),
        compiler_params=_params(("arbitrary",)),
        name="window_gqa",
    )(sink, z, z, z, z, z, z, z, bias_st)


def _conv_kernel(u_ref, gb_ref, gc_ref, up_ref, gcp_ref, un_ref, gcn_ref, w_ref, o_ref, *, tiles_per_seq):
    t = pl.program_id(0) % tiles_per_seq
    p = u_ref[...].astype(F32) * gc_ref[...].astype(F32)
    first = (up_ref[...].astype(F32) * gcp_ref[...].astype(F32))[HALO - 1:HALO]
    last = (un_ref[...].astype(F32) * gcn_ref[...].astype(F32))[0:1]
    first = jnp.where(t == 0, 0.0, first)
    last = jnp.where(t == tiles_per_seq - 1, 0.0, last)
    row = lax.broadcasted_iota(jnp.int32, (CONV_ROWS, 1), 0)
    before = jnp.where(row == 0, first, pltpu.roll(p, 1, axis=0))
    after = jnp.where(row == CONV_ROWS - 1, last, pltpu.roll(p, CONV_ROWS - 1, axis=0))
    y = w_ref[0:1, :] * before + w_ref[1:2, :] * p + w_ref[2:3, :] * after
    o_ref[...] = (gb_ref[...].astype(F32) * y).astype(o_ref.dtype)


def _conv_gate(z, conv_w, seq):
    m = z.shape[0]
    tiles_per_seq = seq // CONV_ROWS
    per = CONV_ROWS // HALO
    last_blk = m // HALO - 1
    main = lambda off: pl.BlockSpec((CONV_ROWS, CONV_COLS), lambda i, c: (i, off // CONV_COLS + c))
    prev = lambda off: pl.BlockSpec((HALO, CONV_COLS),
                                    lambda i, c: (jnp.maximum(i * per - 1, 0), off // CONV_COLS + c))
    nxt = lambda off: pl.BlockSpec((HALO, CONV_COLS),
                                   lambda i, c: (jnp.minimum(i * per + per, last_blk), off // CONV_COLS + c))
    return pl.pallas_call(
        functools.partial(_conv_kernel, tiles_per_seq=tiles_per_seq),
        grid=(m // CONV_ROWS, D_CONV // CONV_COLS),
        in_specs=[main(OFF_U), main(OFF_GB), main(OFF_GC), prev(OFF_U), prev(OFF_GC), nxt(OFF_U), nxt(OFF_GC),
                  pl.BlockSpec((3, CONV_COLS), lambda i, c: (0, c))],
        out_specs=pl.BlockSpec((CONV_ROWS, CONV_COLS), lambda i, c: (i, c)),
        out_shape=jax.ShapeDtypeStruct((m, D_CONV), BF16),
        compiler_params=_params(("arbitrary", "arbitrary")),
        name="conv_gate",
    )(z, z, z, z, z, z, z, conv_w)


def kernel(x, c, rel_bias, norm1_g, norm2_g, w_ada, b_ada, w_in, b_merge, sink, conv_w,
           w_att_proj, w_conv_proj, w_out, w_mlp_in, w_mlp_out, final_g):
    bsz, seq, d = x.shape
    depth = w_ada.shape[0]
    m = bsz * seq
    tiles_per_batch = seq // MM_BM
    xf = x.reshape(m, d)

    c_pad = jnp.zeros((MOD_ROWS, d), F32).at[:bsz].set(c)
    mods = _modulation(c_pad, w_ada, b_ada)[:, :bsz].reshape(depth, bsz, N_MOD, 1, d)
    bias_st = _bias_table(rel_bias).reshape(N_KV_HEADS, GROUP * BLOCK, 3 * BLOCK)

    def gate_map(i, j):
        return (i // tiles_per_batch, 0, j)

    for l in range(depth):
        shift1, scale1, gate1, shift2, scale2, gate2 = [mods[l, :, t] for t in range(N_MOD)]
        b_m = b_merge[l].reshape(1, 2 * d)

        h = _norm_mod(xf, norm1_g[l], scale1, shift1, seq)
        z = _matmul_resident([(h, w_in)], l, [], _ep_plain, D_IN, BF16, "in_proj")
        att = _attention(z, bias_st, sink[l], seq)
        bm = _conv_gate(z, conv_w[l], seq)
        merged = _matmul_resident(
            [(att, w_att_proj), (bm, w_conv_proj)], l,
            [(z, (MM_BM, MM_BN), lambda i, j: (i, OFF_GA // MM_BN + j)),
             (z, (MM_BM, MM_BN), lambda i, j: (i, OFF_GCV // MM_BN + j)),
             (b_m, (1, MM_BN), lambda i, j: (0, j)),
             (b_m, (1, MM_BN), lambda i, j: (0, d // MM_BN + j))],
            _ep_merge, d, BF16, "branch_merge")
        xf = _matmul_resident([(merged, w_out)], l,
                              [(xf, (MM_BM, OUT_BN), lambda i, j: (i, j)), (gate1, (None, 1, OUT_BN), gate_map)],
                              _ep_residual, d, F32, "out_proj", bn=OUT_BN)

        h2 = _norm_mod(xf, norm2_g[l], scale2, shift2, seq)
        a, w_mlp_out_bf16 = _matmul_resident([(h2, w_mlp_in)], l, [], _ep_relu2, D_FF, BF16, "mlp_in",
                                             convert=w_mlp_out)
        xf = _matmul_ksplit_residual(a, w_mlp_out_bf16, xf, gate2, gate_map, "mlp_out")

    return _final_norm(xf, final_g).reshape(bsz, seq, d)
```

```python
import functools

import jax
import jax.numpy as jnp
import numpy as np
from jax import lax
from jax.experimental import pallas as pl
from jax.experimental.pallas import tpu as pltpu

F32 = jnp.float32
BF16 = jnp.bfloat16

D_MODEL = 4096
N_HEADS = 16
HEAD_DIM = 128
N_KV_HEADS = 4
GROUP = N_HEADS // N_KV_HEADS
D_ATT = N_HEADS * HEAD_DIM
D_KV = N_KV_HEADS * HEAD_DIM
WINDOW = 128
BLOCK = 128
N_BUCKETS = 32
MAX_DISTANCE = 128
D_CONV = D_MODEL // 2
D_FF = 4 * D_MODEL
N_MOD = 6
EPS = 1e-6
NEG = -1e30

OFF_Q = 0
OFF_K = OFF_Q + D_ATT
OFF_V = OFF_K + D_KV
OFF_U = OFF_V + D_KV
OFF_GB = OFF_U + D_CONV
OFF_GC = OFF_GB + D_CONV
OFF_GA = OFF_GC + D_CONV
OFF_GCV = OFF_GA + D_MODEL
D_IN = OFF_GCV + D_MODEL

VMEM_LIMIT_BYTES = 56 * 1024 * 1024
VMEM_LIMIT_LARGE = 60 * 1024 * 1024

MM_BM = 1024
MM_BN = 1024
MM_BK = 4096
LANES = 128
NORM_ROWS = 512
NORM_HALF = 256
ATT_ROWS = 512
CONV_ROWS = 512
CONV_COLS = 1024
HALO = 16
MOD_BN = 512
MOD_ROWS = 8


def _params(sem, vmem=VMEM_LIMIT_BYTES):
    return pltpu.CompilerParams(dimension_semantics=sem, vmem_limit_bytes=vmem)


def _mod_kernel(c_ref, w_ref, b_ref, o_ref):
    c = c_ref[...]
    c_act = c * jax.nn.sigmoid(c)
    acc = jnp.dot(c_act.astype(BF16), w_ref[...].astype(BF16), preferred_element_type=F32)
    o_ref[...] = acc + b_ref[...]


def _modulation(c_pad, w_ada, b_ada):
    depth, d, n = w_ada.shape
    return pl.pallas_call(
        _mod_kernel,
        grid=(depth, n // MOD_BN),
        in_specs=[
            pl.BlockSpec((MOD_ROWS, d), lambda l, j: (0, 0)),
            pl.BlockSpec((None, d, MOD_BN), lambda l, j: (l, 0, j)),
            pl.BlockSpec((None, 1, MOD_BN), lambda l, j: (l, 0, j)),
        ],
        out_specs=pl.BlockSpec((None, MOD_ROWS, MOD_BN), lambda l, j: (l, 0, j)),
        out_shape=jax.ShapeDtypeStruct((depth, MOD_ROWS, n), F32),
        compiler_params=_params(("arbitrary", "arbitrary")),
        name="adaln_mod",
    )(c_pad, w_ada, b_ada.reshape(depth, 1, n))


def _inv_rms(x_ref, rstd_ref):
    d = x_ref.shape[1]
    for r in range(NORM_ROWS // NORM_HALF):
        rows = slice(r * NORM_HALF, (r + 1) * NORM_HALF)
        acc = jnp.zeros((NORM_HALF, LANES), F32)
        for c in range(d // LANES):
            x = x_ref[rows, c * LANES:(c + 1) * LANES]
            acc = acc + x * x
        mean_sq = jnp.sum(acc, axis=-1, keepdims=True) / d
        rstd_ref[rows, :] = jnp.broadcast_to(lax.rsqrt(mean_sq + EPS), (NORM_HALF, LANES))


def _norm_mod_kernel(x_ref, g_ref, scale_ref, shift_ref, o_ref, rstd_ref):
    _inv_rms(x_ref, rstd_ref)
    for c in range(x_ref.shape[1] // LANES):
        cols = slice(c * LANES, (c + 1) * LANES)
        y = x_ref[:, cols] * rstd_ref[...]
        h = (y * g_ref[:, cols]) * (1.0 + scale_ref[:, cols]) + shift_ref[:, cols]
        o_ref[:, cols] = h.astype(o_ref.dtype)


def _norm_kernel(x_ref, g_ref, o_ref, rstd_ref):
    _inv_rms(x_ref, rstd_ref)
    for c in range(x_ref.shape[1] // LANES):
        cols = slice(c * LANES, (c + 1) * LANES)
        y = x_ref[:, cols] * rstd_ref[...]
        o_ref[:, cols] = (y * g_ref[:, cols]).astype(o_ref.dtype)


def _norm_mod(x, g, scale, shift, seq):
    m, d = x.shape
    per_batch = seq // NORM_ROWS
    row = pl.BlockSpec((NORM_ROWS, d), lambda i: (i, 0))
    mod = pl.BlockSpec((None, 1, d), lambda i: (i // per_batch, 0, 0))
    return pl.pallas_call(
        _norm_mod_kernel,
        grid=(m // NORM_ROWS,),
        in_specs=[row, pl.BlockSpec((1, d), lambda i: (0, 0)), mod, mod],
        out_specs=row,
        out_shape=jax.ShapeDtypeStruct((m, d), BF16),
        scratch_shapes=[pltpu.VMEM((NORM_ROWS, LANES), F32)],
        compiler_params=_params(("arbitrary",)),
        name="norm_mod",
    )(x, g.reshape(1, d), scale, shift)


def _final_norm(x, g):
    m, d = x.shape
    row = pl.BlockSpec((NORM_ROWS, d), lambda i: (i, 0))
    return pl.pallas_call(
        _norm_kernel,
        grid=(m // NORM_ROWS,),
        in_specs=[row, pl.BlockSpec((1, d), lambda i: (0, 0))],
        out_specs=row,
        out_shape=jax.ShapeDtypeStruct((m, d), F32),
        scratch_shapes=[pltpu.VMEM((NORM_ROWS, LANES), F32)],
        compiler_params=_params(("arbitrary",)),
        name="final_norm",
    )(x, g.reshape(1, d))


def _mm_resident_kernel(*refs, npairs, nextra, nside, epilogue):
    a_refs = refs[:npairs]
    wchunk_refs = refs[npairs:2 * npairs]
    extra = refs[2 * npairs:2 * npairs + nextra]
    side_in = refs[2 * npairs + nextra:2 * npairs + nextra + nside]
    o_ref = refs[2 * npairs + nextra + nside]
    side_out = refs[2 * npairs + nextra + nside + 1:2 * npairs + nextra + 2 * nside + 1]
    wbuf_refs = refs[2 * npairs + nextra + 2 * nside + 1:]
    n = pl.program_id(0)
    i = pl.program_id(1)

    def convert_chunk():
        for wchunk, wbuf in zip(wchunk_refs, wbuf_refs):
            rows = wchunk.shape[0]
            wbuf[n % 2, pl.ds(pl.multiple_of(i * rows, rows), rows), :] = wchunk[...].astype(BF16)

    @pl.when(n == 0)
    def _():
        convert_chunk()

    @pl.when(n > 0)
    def _():
        convert_chunk()
        for src, dst in zip(side_in, side_out):
            dst[...] = src[...].astype(dst.dtype)
        slot = (n + 1) % 2
        accs = [jnp.dot(a[...], wbuf[slot], preferred_element_type=F32) for a, wbuf in zip(a_refs, wbuf_refs)]
        o_ref[...] = epilogue(accs, extra).astype(o_ref.dtype)


def _matmul_resident(pairs, layer, extras, epilogue, n_out, out_dtype, name, bn=MM_BN, convert=None,
                     vmem=VMEM_LIMIT_BYTES):
    m = pairs[0][0].shape[0]
    mt, nt = m // MM_BM, n_out // bn

    def at_tile(f):
        return lambda n, i: f(jnp.where(n > 0, i, 0), jnp.maximum(n - 1, 0))

    a_specs = [pl.BlockSpec((MM_BM, a.shape[1]), at_tile(lambda i, j: (i, 0))) for a, _ in pairs]
    w_specs = [pl.BlockSpec((None, w.shape[1] // mt, bn), lambda n, i: (layer, i, jnp.minimum(n, nt - 1)))
               for _, w in pairs]
    ex_specs = [pl.BlockSpec(bs, at_tile(f)) for _, bs, f in extras]
    out_specs = [pl.BlockSpec((MM_BM, bn), at_tile(lambda i, j: (i, j)))]
    out_shape = [jax.ShapeDtypeStruct((m, n_out), out_dtype)]
    side_specs, side_args = [], []
    if convert is not None:
        _, r, c = convert.shape
        rows = r // (nt * mt)
        side_specs = [pl.BlockSpec((None, rows, c), at_tile(lambda i, j: (layer, j * mt + i, 0)))]
        side_args = [convert]
        out_specs.append(pl.BlockSpec((rows, c), at_tile(lambda i, j: (j * mt + i, 0))))
        out_shape.append(jax.ShapeDtypeStruct((r, c), BF16))
    body = functools.partial(_mm_resident_kernel, npairs=len(pairs), nextra=len(extras), nside=len(side_args),
                             epilogue=epilogue)
    outs = pl.pallas_call(
        body,
        grid=(nt + 1, mt),
        in_specs=a_specs + w_specs + ex_specs + side_specs,
        out_specs=out_specs,
        out_shape=out_shape,
        scratch_shapes=[pltpu.VMEM((2, w.shape[1], bn), BF16) for _, w in pairs],
        compiler_params=_params(("arbitrary", "arbitrary"), vmem),
        name=name,
    )(*[a for a, _ in pairs], *[w for _, w in pairs], *[e for e, _, _ in extras], *side_args)
    return outs[0] if convert is None else outs


def _mm_ksplit_kernel(a_ref, w_ref, x_ref, gate_ref, o_ref, *, nk):
    k = pl.program_id(2)

    def part():
        return jnp.dot(a_ref[...], w_ref[...], preferred_element_type=F32)

    @pl.when(k == 0)
    def _():
        o_ref[...] = part()

    @pl.when((k > 0) & (k < nk - 1))
    def _():
        o_ref[...] += part()

    @pl.when(k == nk - 1)
    def _():
        o_ref[...] = x_ref[...] + gate_ref[...] * (o_ref[...] + part())


def _matmul_ksplit_residual(a, w, x, gate, gate_map, name):
    m, kdim = a.shape
    n_out = w.shape[1]
    nk = kdim // MM_BK
    tile = pl.BlockSpec((MM_BM, MM_BN), lambda i, j, k: (i, j))
    return pl.pallas_call(
        functools.partial(_mm_ksplit_kernel, nk=nk),
        grid=(m // MM_BM, n_out // MM_BN, nk),
        in_specs=[
            pl.BlockSpec((MM_BM, MM_BK), lambda i, j, k: (i, k)),
            pl.BlockSpec((MM_BK, MM_BN), lambda i, j, k: (k, j)),
            tile,
            pl.BlockSpec((None, 1, MM_BN), lambda i, j, k: gate_map(i, j)),
        ],
        out_specs=tile,
        out_shape=jax.ShapeDtypeStruct((m, n_out), F32),
        compiler_params=_params(("arbitrary", "arbitrary", "arbitrary")),
        name=name,
    )(a, w, x, gate)


def _ep_plain(accs, extra):
    return accs[0]


def _ep_relu2(accs, extra):
    return jnp.square(jnp.maximum(accs[0], 0.0))


def _ep_merge(accs, extra):
    glog_att, glog_conv, b_att, b_conv = extra
    g_att = jax.nn.sigmoid(glog_att[...].astype(F32) + b_att[...])
    g_conv = jax.nn.sigmoid(glog_conv[...].astype(F32) + b_conv[...])
    return g_att * accs[0] + g_conv * accs[1]


def _ep_residual(accs, extra):
    x, gate = extra
    return x[...] + gate[...] * accs[0]


def _bias_kernel(rel_ref, bucket_ref, o_ref):
    h = pl.program_id(0)
    bucket = bucket_ref[...]
    acc = jnp.full(bucket.shape, NEG, F32)
    for b in range(N_BUCKETS):
        acc = jnp.where(bucket == b, rel_ref[b, h], acc)
    o_ref[...] = acc


def _t5_bucket(rel):
    nb = N_BUCKETS // 2
    max_exact = nb // 2
    ret = (rel > 0).astype(jnp.int32) * nb
    n = jnp.abs(rel)
    nf = jnp.maximum(n, 1).astype(F32)
    large = max_exact + (jnp.log(nf / max_exact) / np.float32(np.log(MAX_DISTANCE / max_exact))
                         * (nb - max_exact)).astype(jnp.int32)
    large = jnp.minimum(large, nb - 1)
    return ret + jnp.where(n < max_exact, n, large)


def _bias_table(rel_bias):
    q_idx = jnp.arange(BLOCK)[:, None]
    s_idx = jnp.arange(3 * BLOCK)[None, :]
    rel = s_idx - BLOCK - q_idx
    bucket = jnp.where(jnp.abs(rel) <= WINDOW, _t5_bucket(rel), -1).astype(jnp.int32)
    return pl.pallas_call(
        _bias_kernel,
        grid=(N_HEADS,),
        in_specs=[
            pl.BlockSpec(memory_space=pltpu.SMEM),
            pl.BlockSpec((BLOCK, 3 * BLOCK), lambda h: (0, 0)),
        ],
        out_specs=pl.BlockSpec((None, BLOCK, 3 * BLOCK), lambda h: (h, 0, 0)),
        out_shape=jax.ShapeDtypeStruct((N_HEADS, BLOCK, 3 * BLOCK), F32),
        compiler_params=_params(("arbitrary",)),
        name="rel_bias_table",
    )(rel_bias, bucket)


def _attn_kernel(sink_ref, q_ref, kc_ref, vc_ref, kp_ref, vp_ref, kn_ref, vn_ref, bias_ref, o_ref,
                 *, tiles_per_seq):
    t = pl.program_id(0) % tiles_per_seq
    nsub = ATT_ROWS // BLOCK
    gb = GROUP * BLOCK
    col = lax.broadcasted_iota(jnp.int32, (1, 3 * BLOCK), 1)
    head_row = lax.broadcasted_iota(jnp.int32, (gb, 1), 0)
    scale = np.float32(HEAD_DIM ** -0.5)

    def band(cur_ref, prev_ref, next_ref, j, cols):
        if j == 0:
            return jnp.concatenate([prev_ref[:, cols], cur_ref[0:2 * BLOCK, cols]], axis=0)
        if j == nsub - 1:
            return jnp.concatenate([cur_ref[(j - 1) * BLOCK:(j + 1) * BLOCK, cols], next_ref[:, cols]], axis=0)
        return cur_ref[(j - 1) * BLOCK:(j + 2) * BLOCK, cols]

    for j in range(nsub):
        rows = slice(j * BLOCK, (j + 1) * BLOCK)
        if j == 0:
            dead = (col < BLOCK) & (t == 0)
        elif j == nsub - 1:
            dead = (col >= 2 * BLOCK) & (t == tiles_per_seq - 1)
        else:
            dead = None
        for kv in range(N_KV_HEADS):
            cols = slice(kv * HEAD_DIM, (kv + 1) * HEAD_DIM)
            q = jnp.concatenate(
                [q_ref[rows, (kv * GROUP + g) * HEAD_DIM:(kv * GROUP + g + 1) * HEAD_DIM] for g in range(GROUP)],
                axis=0)
            kb = band(kc_ref, kp_ref, kn_ref, j, cols)
            vb = band(vc_ref, vp_ref, vn_ref, j, cols)
            s = lax.dot_general(q, kb, (((1,), (1,)), ((), ())), preferred_element_type=F32)
            s = s * scale + bias_ref[kv]
            if dead is not None:
                s = jnp.where(dead, NEG, s)
            sink = jnp.full((gb, 1), sink_ref[kv * GROUP], F32)
            for g in range(1, GROUP):
                sink = jnp.where(head_row >= g * BLOCK, sink_ref[kv * GROUP + g], sink)
            mx = jnp.maximum(jnp.max(s, axis=-1, keepdims=True), sink)
            p = jnp.exp(s - mx)
            denom = jnp.sum(p, axis=-1, keepdims=True) + jnp.exp(sink - mx)
            o = jnp.dot(p.astype(BF16), vb, preferred_element_type=F32) / denom
            for g in range(GROUP):
                h = kv * GROUP + g
                o_ref[rows, h * HEAD_DIM:(h + 1) * HEAD_DIM] = o[g * BLOCK:(g + 1) * BLOCK].astype(o_ref.dtype)


def _attention(z, bias_st, sink, seq):
    m = z.shape[0]
    tiles_per_seq = seq // ATT_ROWS
    sub = ATT_ROWS // BLOCK
    last_blk = m // BLOCK - 1
    kcol, vcol = OFF_K // D_KV, OFF_V // D_KV
    cur = lambda c: pl.BlockSpec((ATT_ROWS, D_KV), lambda i: (i, c))
    prev = lambda c: pl.BlockSpec((BLOCK, D_KV), lambda i: (jnp.maximum(i * sub - 1, 0), c))
    nxt = lambda c: pl.BlockSpec((BLOCK, D_KV), lambda i: (jnp.minimum(i * sub + sub, last_blk), c))
    return pl.pallas_call(
        functools.partial(_attn_kernel, tiles_per_seq=tiles_per_seq),
        grid=(m // ATT_ROWS,),
        in_specs=[
            pl.BlockSpec(memory_space=pltpu.SMEM),
            pl.BlockSpec((ATT_ROWS, D_ATT), lambda i: (i, OFF_Q // D_ATT)),
            cur(kcol), cur(vcol), prev(kcol), prev(vcol), nxt(kcol), nxt(vcol),
            pl.BlockSpec((N_KV_HEADS, GROUP * BLOCK, 3 * BLOCK), lambda i: (0, 0, 0)),
        ],
        out_specs=pl.BlockSpec((ATT_ROWS, D_ATT), lambda i: (i, 0)),
        out_shape=jax.ShapeDtypeStruct((m, D_ATT), BF16),
        compiler_params=_params(("arbitrary",)),
        name="window_gqa",
    )(sink, z, z, z, z, z, z, z, bias_st)


def _conv_kernel(u_ref, gb_ref, gc_ref, up_ref, gcp_ref, un_ref, gcn_ref, w_ref, o_ref, *, tiles_per_seq):
    t = pl.program_id(0) % tiles_per_seq
    p = u_ref[...].astype(F32) * gc_ref[...].astype(F32)
    first = (up_ref[...].astype(F32) * gcp_ref[...].astype(F32))[HALO - 1:HALO]
    last = (un_ref[...].astype(F32) * gcn_ref[...].astype(F32))[0:1]
    first = jnp.where(t == 0, 0.0, first)
    last = jnp.where(t == tiles_per_seq - 1, 0.0, last)
    row = lax.broadcasted_iota(jnp.int32, (CONV_ROWS, 1), 0)
    before = jnp.where(row == 0, first, pltpu.roll(p, 1, axis=0))
    after = jnp.where(row == CONV_ROWS - 1, last, pltpu.roll(p, CONV_ROWS - 1, axis=0))
    y = w_ref[0:1, :] * before + w_ref[1:2, :] * p + w_ref[2:3, :] * after
    o_ref[...] = (gb_ref[...].astype(F32) * y).astype(o_ref.dtype)


def _conv_gate(z, conv_w, seq):
    m = z.shape[0]
    tiles_per_seq = seq // CONV_ROWS
    per = CONV_ROWS // HALO
    last_blk = m // HALO - 1
    main = lambda off: pl.BlockSpec((CONV_ROWS, CONV_COLS), lambda i, c: (i, off // CONV_COLS + c))
    prev = lambda off: pl.BlockSpec((HALO, CONV_COLS),
                                    lambda i, c: (jnp.maximum(i * per - 1, 0), off // CONV_COLS + c))
    nxt = lambda off: pl.BlockSpec((HALO, CONV_COLS),
                                   lambda i, c: (jnp.minimum(i * per + per, last_blk), off // CONV_COLS + c))
    return pl.pallas_call(
        functools.partial(_conv_kernel, tiles_per_seq=tiles_per_seq),
        grid=(m // CONV_ROWS, D_CONV // CONV_COLS),
        in_specs=[main(OFF_U), main(OFF_GB), main(OFF_GC), prev(OFF_U), prev(OFF_GC), nxt(OFF_U), nxt(OFF_GC),
                  pl.BlockSpec((3, CONV_COLS), lambda i, c: (0, c))],
        out_specs=pl.BlockSpec((CONV_ROWS, CONV_COLS), lambda i, c: (i, c)),
        out_shape=jax.ShapeDtypeStruct((m, D_CONV), BF16),
        compiler_params=_params(("arbitrary", "arbitrary")),
        name="conv_gate",
    )(z, z, z, z, z, z, z, conv_w)


def kernel(x, c, rel_bias, norm1_g, norm2_g, w_ada, b_ada, w_in, b_merge, sink, conv_w,
           w_att_proj, w_conv_proj, w_out, w_mlp_in, w_mlp_out, final_g):
    bsz, seq, d = x.shape
    depth = w_ada.shape[0]
    m = bsz * seq
    tiles_per_batch = seq // MM_BM
    xf = x.reshape(m, d)

    c_pad = jnp.zeros((MOD_ROWS, d), F32).at[:bsz].set(c)
    mods = _modulation(c_pad, w_ada, b_ada)[:, :bsz].reshape(depth, bsz, N_MOD, 1, d)
    bias_st = _bias_table(rel_bias).reshape(N_KV_HEADS, GROUP * BLOCK, 3 * BLOCK)

    def gate_map(i, j):
        return (i // tiles_per_batch, 0, j)

    for l in range(depth):
        shift1, scale1, gate1, shift2, scale2, gate2 = [mods[l, :, t] for t in range(N_MOD)]
        b_m = b_merge[l].reshape(1, 2 * d)

        h = _norm_mod(xf, norm1_g[l], scale1, shift1, seq)
        z = _matmul_resident([(h, w_in)], l, [], _ep_plain, D_IN, BF16, "in_proj")
        att = _attention(z, bias_st, sink[l], seq)
        bm = _conv_gate(z, conv_w[l], seq)
        merged = _matmul_resident(
            [(att, w_att_proj), (bm, w_conv_proj)], l,
            [(z, (MM_BM, MM_BN), lambda i, j: (i, OFF_GA // MM_BN + j)),
             (z, (MM_BM, MM_BN), lambda i, j: (i, OFF_GCV // MM_BN + j)),
             (b_m, (1, MM_BN), lambda i, j: (0, j)),
             (b_m, (1, MM_BN), lambda i, j: (0, d // MM_BN + j))],
            _ep_merge, d, BF16, "branch_merge")
        xf = _matmul_resident([(merged, w_out)], l,
                              [(xf, (MM_BM, MM_BN), lambda i, j: (i, j)), (gate1, (None, 1, MM_BN), gate_map)],
                              _ep_residual, d, F32, "out_proj", vmem=VMEM_LIMIT_LARGE)

        h2 = _norm_mod(xf, norm2_g[l], scale2, shift2, seq)
        a, w_mlp_out_bf16 = _matmul_resident([(h2, w_mlp_in)], l, [], _ep_relu2, D_FF, BF16, "mlp_in",
                                             convert=w_mlp_out)
        xf = _matmul_ksplit_residual(a, w_mlp_out_bf16, xf, gate2, gate_map, "mlp_out")

    return _final_norm(xf, final_g).reshape(bsz, seq, d)
```

```python
import functools

import jax
import jax.numpy as jnp
import numpy as np
from jax import lax
from jax.experimental import pallas as pl
from jax.experimental.pallas import tpu as pltpu

F32 = jnp.float32
BF16 = jnp.bfloat16

D_MODEL = 4096
N_HEADS = 16
HEAD_DIM = 128
N_KV_HEADS = 4
GROUP = N_HEADS // N_KV_HEADS
D_ATT = N_HEADS * HEAD_DIM
D_KV = N_KV_HEADS * HEAD_DIM
WINDOW = 128
BLOCK = 128
N_BUCKETS = 32
MAX_DISTANCE = 128
D_CONV = D_MODEL // 2
D_FF = 4 * D_MODEL
N_MOD = 6
EPS = 1e-6
NEG = -1e30

OFF_Q = 0
OFF_K = OFF_Q + D_ATT
OFF_V = OFF_K + D_KV
OFF_U = OFF_V + D_KV
OFF_GB = OFF_U + D_CONV
OFF_GC = OFF_GB + D_CONV
OFF_GA = OFF_GC + D_CONV
OFF_GCV = OFF_GA + D_MODEL
D_IN = OFF_GCV + D_MODEL

VMEM_LIMIT_BYTES = 56 * 1024 * 1024
VMEM_LIMIT_LARGE = 60 * 1024 * 1024

MM_BM = 1024
MM_BN = 1024
MM_BK = 4096
LANES = 128
NORM_ROWS = 512
NORM_HALF = 256
ATT_ROWS = 512
ATT_HEADS = 4
CONV_ROWS = 512
CONV_COLS = 1024
HALO = 16
MOD_BN = 512
MOD_ROWS = 8


def _params(sem, vmem=VMEM_LIMIT_BYTES):
    return pltpu.CompilerParams(dimension_semantics=sem, vmem_limit_bytes=vmem)


def _mod_block(c_ref, w_ref, b_ref):
    c = c_ref[...]
    c_act = c * jax.nn.sigmoid(c)
    acc = jnp.dot(c_act.astype(BF16), w_ref[...].astype(BF16), preferred_element_type=F32)
    return acc + b_ref[...]


def _mod_kernel(c_ref, w_ref, b_ref, o_ref):
    o_ref[...] = _mod_block(c_ref, w_ref, b_ref)


def _mod_specs(d, bn, layer, col):
    in_specs = [
        pl.BlockSpec((MOD_ROWS, d), lambda *g: (0, 0)),
        pl.BlockSpec((None, d, bn), lambda *g: (layer, 0, col(*g))),
        pl.BlockSpec((None, 1, bn), lambda *g: (layer, 0, col(*g))),
    ]
    return in_specs, pl.BlockSpec((MOD_ROWS, bn), lambda *g: (0, col(*g)))


def _modulation(c_pad, w_ada, b_ada, layer):
    _, d, n = w_ada.shape
    in_specs, out_spec = _mod_specs(d, MOD_BN, layer, lambda j: j)
    return pl.pallas_call(
        _mod_kernel,
        grid=(n // MOD_BN,),
        in_specs=in_specs,
        out_specs=out_spec,
        out_shape=jax.ShapeDtypeStruct((MOD_ROWS, n), F32),
        compiler_params=_params(("arbitrary",)),
        name="adaln_mod",
    )(c_pad, w_ada, b_ada)


def _inv_rms(x_ref, rstd_ref):
    d = x_ref.shape[1]
    for r in range(NORM_ROWS // NORM_HALF):
        rows = slice(r * NORM_HALF, (r + 1) * NORM_HALF)
        acc = jnp.zeros((NORM_HALF, LANES), F32)
        for c in range(d // LANES):
            x = x_ref[rows, c * LANES:(c + 1) * LANES]
            acc = acc + x * x
        mean_sq = jnp.sum(acc, axis=-1, keepdims=True) / d
        rstd_ref[rows, :] = jnp.broadcast_to(lax.rsqrt(mean_sq + EPS), (NORM_HALF, LANES))


def _norm_mod_kernel(x_ref, g_ref, scale_ref, shift_ref, o_ref, rstd_ref):
    _inv_rms(x_ref, rstd_ref)
    for c in range(x_ref.shape[1] // LANES):
        cols = slice(c * LANES, (c + 1) * LANES)
        y = x_ref[:, cols] * rstd_ref[...]
        h = (y * g_ref[:, cols]) * (1.0 + scale_ref[:, cols]) + shift_ref[:, cols]
        o_ref[:, cols] = h.astype(o_ref.dtype)


def _norm_kernel(x_ref, g_ref, o_ref, rstd_ref):
    _inv_rms(x_ref, rstd_ref)
    for c in range(x_ref.shape[1] // LANES):
        cols = slice(c * LANES, (c + 1) * LANES)
        y = x_ref[:, cols] * rstd_ref[...]
        o_ref[:, cols] = (y * g_ref[:, cols]).astype(o_ref.dtype)


def _norm_mod(x, g, scale, shift, seq):
    m, d = x.shape
    per_batch = seq // NORM_ROWS
    row = pl.BlockSpec((NORM_ROWS, d), lambda i: (i, 0))
    mod = pl.BlockSpec((None, 1, d), lambda i: (i // per_batch, 0, 0))
    return pl.pallas_call(
        _norm_mod_kernel,
        grid=(m // NORM_ROWS,),
        in_specs=[row, pl.BlockSpec((1, d), lambda i: (0, 0)), mod, mod],
        out_specs=row,
        out_shape=jax.ShapeDtypeStruct((m, d), BF16),
        scratch_shapes=[pltpu.VMEM((NORM_ROWS, LANES), F32)],
        compiler_params=_params(("arbitrary",)),
        name="norm_mod",
    )(x, g.reshape(1, d), scale, shift)


def _final_norm(x, g):
    m, d = x.shape
    row = pl.BlockSpec((NORM_ROWS, d), lambda i: (i, 0))
    return pl.pallas_call(
        _norm_kernel,
        grid=(m // NORM_ROWS,),
        in_specs=[row, pl.BlockSpec((1, d), lambda i: (0, 0))],
        out_specs=row,
        out_shape=jax.ShapeDtypeStruct((m, d), F32),
        scratch_shapes=[pltpu.VMEM((NORM_ROWS, LANES), F32)],
        compiler_params=_params(("arbitrary",)),
        name="final_norm",
    )(x, g.reshape(1, d))


def _mm_resident_kernel(*refs, npairs, nextra, nside, epilogue):
    a_refs = refs[:npairs]
    wchunk_refs = refs[npairs:2 * npairs]
    extra = refs[2 * npairs:2 * npairs + nextra]
    side_in = refs[2 * npairs + nextra:2 * npairs + nextra + nside]
    o_ref = refs[2 * npairs + nextra + nside]
    side_out = refs[2 * npairs + nextra + nside + 1:2 * npairs + nextra + 2 * nside + 1]
    wbuf_refs = refs[2 * npairs + nextra + 2 * nside + 1:]
    n = pl.program_id(0)
    i = pl.program_id(1)

    def convert_chunk():
        for wchunk, wbuf in zip(wchunk_refs, wbuf_refs):
            rows = wchunk.shape[0]
            wbuf[n % 2, pl.ds(pl.multiple_of(i * rows, rows), rows), :] = wchunk[...].astype(BF16)

    @pl.when(n == 0)
    def _():
        convert_chunk()

    @pl.when(n > 0)
    def _():
        convert_chunk()
        for src, dst in zip(side_in, side_out):
            dst[...] = src[...].astype(dst.dtype)
        slot = (n + 1) % 2
        accs = [jnp.dot(a[...], wbuf[slot], preferred_element_type=F32) for a, wbuf in zip(a_refs, wbuf_refs)]
        o_ref[...] = epilogue(accs, extra).astype(o_ref.dtype)


def _matmul_resident(pairs, layer, extras, epilogue, n_out, out_dtype, name, bn=MM_BN, convert=None,
                     vmem=VMEM_LIMIT_BYTES):
    m = pairs[0][0].shape[0]
    mt, nt = m // MM_BM, n_out // bn

    def at_tile(f):
        return lambda n, i: f(jnp.where(n > 0, i, 0), jnp.maximum(n - 1, 0))

    a_specs = [pl.BlockSpec((MM_BM, a.shape[1]), at_tile(lambda i, j: (i, 0))) for a, _ in pairs]
    w_specs = [pl.BlockSpec((None, w.shape[1] // mt, bn), lambda n, i: (layer, i, jnp.minimum(n, nt - 1)))
               for _, w in pairs]
    ex_specs = [pl.BlockSpec(bs, at_tile(f)) for _, bs, f in extras]
    out_specs = [pl.BlockSpec((MM_BM, bn), at_tile(lambda i, j: (i, j)))]
    out_shape = [jax.ShapeDtypeStruct((m, n_out), out_dtype)]
    side_specs, side_args = [], []
    if convert is not None:
        _, r, c = convert.shape
        rows = r // (nt * mt)
        side_specs = [pl.BlockSpec((None, rows, c), at_tile(lambda i, j: (layer, j * mt + i, 0)))]
        side_args = [convert]
        out_specs.append(pl.BlockSpec((rows, c), at_tile(lambda i, j: (j * mt + i, 0))))
        out_shape.append(jax.ShapeDtypeStruct((r, c), BF16))
    body = functools.partial(_mm_resident_kernel, npairs=len(pairs), nextra=len(extras), nside=len(side_args),
                             epilogue=epilogue)
    outs = pl.pallas_call(
        body,
        grid=(nt + 1, mt),
        in_specs=a_specs + w_specs + ex_specs + side_specs,
        out_specs=out_specs,
        out_shape=out_shape,
        scratch_shapes=[pltpu.VMEM((2, w.shape[1], bn), BF16) for _, w in pairs],
        compiler_params=_params(("arbitrary", "arbitrary"), vmem),
        name=name,
    )(*[a for a, _ in pairs], *[w for _, w in pairs], *[e for e, _, _ in extras], *side_args)
    return outs[0] if convert is None else outs


def _mm_ksplit_kernel(a_ref, w_ref, x_ref, gate_ref, o_ref, *, nk):
    k = pl.program_id(2)

    def part():
        return jnp.dot(a_ref[...], w_ref[...], preferred_element_type=F32)

    @pl.when(k == 0)
    def _():
        o_ref[...] = part()

    @pl.when((k > 0) & (k < nk - 1))
    def _():
        o_ref[...] += part()

    @pl.when(k == nk - 1)
    def _():
        o_ref[...] = x_ref[...] + gate_ref[...] * (o_ref[...] + part())


def _matmul_ksplit_residual(a, w, x, gate, gate_map, name):
    m, kdim = a.shape
    n_out = w.shape[1]
    nk = kdim // MM_BK
    tile = pl.BlockSpec((MM_BM, MM_BN), lambda i, j, k: (i, j))
    return pl.pallas_call(
        functools.partial(_mm_ksplit_kernel, nk=nk),
        grid=(m // MM_BM, n_out // MM_BN, nk),
        in_specs=[
            pl.BlockSpec((MM_BM, MM_BK), lambda i, j, k: (i, k)),
            pl.BlockSpec((MM_BK, MM_BN), lambda i, j, k: (k, j)),
            tile,
            pl.BlockSpec((None, 1, MM_BN), lambda i, j, k: gate_map(i, j)),
        ],
        out_specs=tile,
        out_shape=jax.ShapeDtypeStruct((m, n_out), F32),
        compiler_params=_params(("arbitrary", "arbitrary", "arbitrary")),
        name=name,
    )(a, w, x, gate)


def _ep_plain(accs, extra):
    return accs[0]


def _ep_relu2(accs, extra):
    return jnp.square(jnp.maximum(accs[0], 0.0))


def _ep_merge(accs, extra):
    glog_att, glog_conv, b_att, b_conv = extra
    g_att = jax.nn.sigmoid(glog_att[...].astype(F32) + b_att[...])
    g_conv = jax.nn.sigmoid(glog_conv[...].astype(F32) + b_conv[...])
    return g_att * accs[0] + g_conv * accs[1]


def _ep_residual(accs, extra):
    x, gate = extra
    return x[...] + gate[...] * accs[0]


def _bias_kernel(rel_ref, bucket_ref, o_ref):
    h = pl.program_id(0)
    bucket = bucket_ref[...]
    acc = jnp.full(bucket.shape, NEG, F32)
    for b in range(N_BUCKETS):
        acc = jnp.where(bucket == b, rel_ref[b, h], acc)
    o_ref[...] = acc


def _t5_bucket(rel):
    nb = N_BUCKETS // 2
    max_exact = nb // 2
    ret = (rel > 0).astype(jnp.int32) * nb
    n = jnp.abs(rel)
    nf = jnp.maximum(n, 1).astype(F32)
    large = max_exact + (jnp.log(nf / max_exact) / np.float32(np.log(MAX_DISTANCE / max_exact))
                         * (nb - max_exact)).astype(jnp.int32)
    large = jnp.minimum(large, nb - 1)
    return ret + jnp.where(n < max_exact, n, large)


def _bias_table(rel_bias):
    q_idx = jnp.arange(BLOCK)[:, None]
    s_idx = jnp.arange(3 * BLOCK)[None, :]
    rel = s_idx - BLOCK - q_idx
    bucket = jnp.where(jnp.abs(rel) <= WINDOW, _t5_bucket(rel), -1).astype(jnp.int32)
    return pl.pallas_call(
        _bias_kernel,
        grid=(N_HEADS,),
        in_specs=[
            pl.BlockSpec(memory_space=pltpu.SMEM),
            pl.BlockSpec((BLOCK, 3 * BLOCK), lambda h: (0, 0)),
        ],
        out_specs=pl.BlockSpec((None, BLOCK, 3 * BLOCK), lambda h: (h, 0, 0)),
        out_shape=jax.ShapeDtypeStruct((N_HEADS, BLOCK, 3 * BLOCK), F32),
        compiler_params=_params(("arbitrary",)),
        name="rel_bias_table",
    )(rel_bias, bucket)


def _attn_kernel(sink_ref, q_ref, kc_ref, vc_ref, kp_ref, vp_ref, kn_ref, vn_ref, bias_ref, *rest,
                 tiles_per_seq):
    if len(rest) == 1:
        (o_ref,) = rest
    else:
        c_ref, w_ref, b_ref, o_ref, mod_ref = rest
        mod_ref[...] = _mod_block(c_ref, w_ref, b_ref)
    t = pl.program_id(0) % tiles_per_seq
    nsub = ATT_ROWS // BLOCK
    col = lax.broadcasted_iota(jnp.int32, (1, 3 * BLOCK), 1)
    head_row = lax.broadcasted_iota(jnp.int32, (ATT_HEADS * BLOCK, 1), 0)
    scale = np.float32(HEAD_DIM ** -0.5)

    def band(cur_ref, prev_ref, next_ref, j, cols):
        if j == 0:
            return jnp.concatenate([prev_ref[:, cols], cur_ref[0:2 * BLOCK, cols]], axis=0)
        if j == nsub - 1:
            return jnp.concatenate([cur_ref[(j - 1) * BLOCK:(j + 1) * BLOCK, cols], next_ref[:, cols]], axis=0)
        return cur_ref[(j - 1) * BLOCK:(j + 2) * BLOCK, cols]

    def dead_keys(j):
        if j == 0:
            return (col < BLOCK) & (t == 0)
        if j == nsub - 1:
            return (col >= 2 * BLOCK) & (t == tiles_per_seq - 1)
        return None

    def scores(j, kv, part):
        rows = slice(j * BLOCK, (j + 1) * BLOCK)
        heads = [kv * GROUP + part * ATT_HEADS + g for g in range(ATT_HEADS)]
        q = jnp.concatenate([q_ref[rows, h * HEAD_DIM:(h + 1) * HEAD_DIM] for h in heads], axis=0)
        kb = band(kc_ref, kp_ref, kn_ref, j, slice(kv * HEAD_DIM, (kv + 1) * HEAD_DIM))
        return lax.dot_general(q, kb, (((1,), (1,)), ((), ())), preferred_element_type=F32)

    def softmax(j, kv, part, s):
        heads = [kv * GROUP + part * ATT_HEADS + g for g in range(ATT_HEADS)]
        s = s * scale + bias_ref[kv, part * ATT_HEADS * BLOCK:(part + 1) * ATT_HEADS * BLOCK, :]
        dead = dead_keys(j)
        if dead is not None:
            s = jnp.where(dead, NEG, s)
        sink = jnp.full((ATT_HEADS * BLOCK, 1), sink_ref[heads[0]], F32)
        for g in range(1, ATT_HEADS):
            sink = jnp.where(head_row >= g * BLOCK, sink_ref[heads[g]], sink)
        mx = jnp.maximum(jnp.max(s, axis=-1, keepdims=True), sink)
        p = jnp.exp(s - mx)
        denom = jnp.sum(p, axis=-1, keepdims=True) + jnp.exp(sink - mx)
        return p.astype(BF16), denom

    def finish(j, kv, part, p, denom):
        rows = slice(j * BLOCK, (j + 1) * BLOCK)
        vb = band(vc_ref, vp_ref, vn_ref, j, slice(kv * HEAD_DIM, (kv + 1) * HEAD_DIM))
        o = jnp.dot(p, vb, preferred_element_type=F32) / denom
        for g in range(ATT_HEADS):
            h = kv * GROUP + part * ATT_HEADS + g
            o_ref[rows, h * HEAD_DIM:(h + 1) * HEAD_DIM] = o[g * BLOCK:(g + 1) * BLOCK].astype(o_ref.dtype)

    items = [(j, kv, part) for j in range(nsub) for kv in range(N_KV_HEADS) for part in range(GROUP // ATT_HEADS)]
    s_next = scores(*items[0])
    for idx, item in enumerate(items):
        s_cur = s_next
        if idx + 1 < len(items):
            s_next = scores(*items[idx + 1])
        p, denom = softmax(*item, s_cur)
        finish(*item, p, denom)


def _attention(z, bias_st, sink, seq, next_mod=None):
    m = z.shape[0]
    steps = m // ATT_ROWS
    tiles_per_seq = seq // ATT_ROWS
    sub = ATT_ROWS // BLOCK
    last_blk = m // BLOCK - 1
    kcol, vcol = OFF_K // D_KV, OFF_V // D_KV
    cur = lambda c: pl.BlockSpec((ATT_ROWS, D_KV), lambda i: (i, c))
    prev = lambda c: pl.BlockSpec((BLOCK, D_KV), lambda i: (jnp.maximum(i * sub - 1, 0), c))
    nxt = lambda c: pl.BlockSpec((BLOCK, D_KV), lambda i: (jnp.minimum(i * sub + sub, last_blk), c))
    in_specs = [
        pl.BlockSpec(memory_space=pltpu.SMEM),
        pl.BlockSpec((ATT_ROWS, D_ATT), lambda i: (i, OFF_Q // D_ATT)),
        cur(kcol), cur(vcol), prev(kcol), prev(vcol), nxt(kcol), nxt(vcol),
        pl.BlockSpec((N_KV_HEADS, GROUP * BLOCK, 3 * BLOCK), lambda i: (0, 0, 0)),
    ]
    args = [sink, z, z, z, z, z, z, z, bias_st]
    out_specs = [pl.BlockSpec((ATT_ROWS, D_ATT), lambda i: (i, 0))]
    out_shape = [jax.ShapeDtypeStruct((m, D_ATT), BF16)]
    if next_mod is not None:
        c_pad, w_ada, b_ada, layer = next_mod
        _, d, n = w_ada.shape
        mod_in, mod_out = _mod_specs(d, n // steps, layer, lambda i: i)
        in_specs += mod_in
        args += [c_pad, w_ada, b_ada]
        out_specs.append(mod_out)
        out_shape.append(jax.ShapeDtypeStruct((MOD_ROWS, n), F32))
    outs = pl.pallas_call(
        functools.partial(_attn_kernel, tiles_per_seq=tiles_per_seq),
        grid=(steps,),
        in_specs=in_specs,
        out_specs=out_specs,
        out_shape=out_shape,
        compiler_params=_params(("arbitrary",)),
        name="window_gqa",
    )(*args)
    return (outs[0], None) if next_mod is None else outs


def _conv_kernel(u_ref, gb_ref, gc_ref, up_ref, gcp_ref, un_ref, gcn_ref, w_ref, o_ref, *, tiles_per_seq):
    t = pl.program_id(0) % tiles_per_seq
    p = u_ref[...].astype(F32) * gc_ref[...].astype(F32)
    first = (up_ref[...].astype(F32) * gcp_ref[...].astype(F32))[HALO - 1:HALO]
    last = (un_ref[...].astype(F32) * gcn_ref[...].astype(F32))[0:1]
    first = jnp.where(t == 0, 0.0, first)
    last = jnp.where(t == tiles_per_seq - 1, 0.0, last)
    row = lax.broadcasted_iota(jnp.int32, (CONV_ROWS, 1), 0)
    before = jnp.where(row == 0, first, pltpu.roll(p, 1, axis=0))
    after = jnp.where(row == CONV_ROWS - 1, last, pltpu.roll(p, CONV_ROWS - 1, axis=0))
    y = w_ref[0:1, :] * before + w_ref[1:2, :] * p + w_ref[2:3, :] * after
    o_ref[...] = (gb_ref[...].astype(F32) * y).astype(o_ref.dtype)


def _conv_gate(z, conv_w, seq):
    m = z.shape[0]
    tiles_per_seq = seq // CONV_ROWS
    per = CONV_ROWS // HALO
    last_blk = m // HALO - 1
    main = lambda off: pl.BlockSpec((CONV_ROWS, CONV_COLS), lambda i, c: (i, off // CONV_COLS + c))
    prev = lambda off: pl.BlockSpec((HALO, CONV_COLS),
                                    lambda i, c: (jnp.maximum(i * per - 1, 0), off // CONV_COLS + c))
    nxt = lambda off: pl.BlockSpec((HALO, CONV_COLS),
                                   lambda i, c: (jnp.minimum(i * per + per, last_blk), off // CONV_COLS + c))
    return pl.pallas_call(
        functools.partial(_conv_kernel, tiles_per_seq=tiles_per_seq),
        grid=(m // CONV_ROWS, D_CONV // CONV_COLS),
        in_specs=[main(OFF_U), main(OFF_GB), main(OFF_GC), prev(OFF_U), prev(OFF_GC), nxt(OFF_U), nxt(OFF_GC),
                  pl.BlockSpec((3, CONV_COLS), lambda i, c: (0, c))],
        out_specs=pl.BlockSpec((CONV_ROWS, CONV_COLS), lambda i, c: (i, c)),
        out_shape=jax.ShapeDtypeStruct((m, D_CONV), BF16),
        compiler_params=_params(("arbitrary", "arbitrary")),
        name="conv_gate",
    )(z, z, z, z, z, z, z, conv_w)


def kernel(x, c, rel_bias, norm1_g, norm2_g, w_ada, b_ada, w_in, b_merge, sink, conv_w,
           w_att_proj, w_conv_proj, w_out, w_mlp_in, w_mlp_out, final_g):
    bsz, seq, d = x.shape
    depth = w_ada.shape[0]
    m = bsz * seq
    tiles_per_batch = seq // MM_BM
    xf = x.reshape(m, d)

    c_pad = jnp.zeros((MOD_ROWS, d), F32).at[:bsz].set(c)
    b_ada3 = b_ada.reshape(depth, 1, N_MOD * d)
    mod_rows = _modulation(c_pad, w_ada, b_ada3, 0)
    bias_st = _bias_table(rel_bias).reshape(N_KV_HEADS, GROUP * BLOCK, 3 * BLOCK)

    def gate_map(i, j):
        return (i // tiles_per_batch, 0, j)

    for l in range(depth):
        mods = mod_rows[:bsz].reshape(bsz, N_MOD, 1, d)
        shift1, scale1, gate1, shift2, scale2, gate2 = [mods[:, t] for t in range(N_MOD)]
        b_m = b_merge[l].reshape(1, 2 * d)

        h = _norm_mod(xf, norm1_g[l], scale1, shift1, seq)
        z = _matmul_resident([(h, w_in)], l, [], _ep_plain, D_IN, BF16, "in_proj")
        next_mod = (c_pad, w_ada, b_ada3, l + 1) if l + 1 < depth else None
        att, mod_rows = _attention(z, bias_st, sink[l], seq, next_mod)
        bm = _conv_gate(z, conv_w[l], seq)
        merged = _matmul_resident(
            [(att, w_att_proj), (bm, w_conv_proj)], l,
            [(z, (MM_BM, MM_BN), lambda i, j: (i, OFF_GA // MM_BN + j)),
             (z, (MM_BM, MM_BN), lambda i, j: (i, OFF_GCV // MM_BN + j)),
             (b_m, (1, MM_BN), lambda i, j: (0, j)),
             (b_m, (1, MM_BN), lambda i, j: (0, d // MM_BN + j))],
            _ep_merge, d, BF16, "branch_merge")
        xf = _matmul_resident([(merged, w_out)], l,
                              [(xf, (MM_BM, MM_BN), lambda i, j: (i, j)), (gate1, (None, 1, MM_BN), gate_map)],
                              _ep_residual, d, F32, "out_proj", vmem=VMEM_LIMIT_LARGE)

        h2 = _norm_mod(xf, norm2_g[l], scale2, shift2, seq)
        a, w_mlp_out_bf16 = _matmul_resident([(h2, w_mlp_in)], l, [], _ep_relu2, D_FF, BF16, "mlp_in",
                                             convert=w_mlp_out)
        xf = _matmul_ksplit_residual(a, w_mlp_out_bf16, xf, gate2, gate_map, "mlp_out")

    return _final_norm(xf, final_g).reshape(bsz, seq, d)
```

```python
import functools

import jax
import jax.numpy as jnp
import numpy as np
from jax import lax
from jax.experimental import pallas as pl
from jax.experimental.pallas import tpu as pltpu

F32 = jnp.float32
BF16 = jnp.bfloat16

D_MODEL = 4096
N_HEADS = 16
HEAD_DIM = 128
N_KV_HEADS = 4
GROUP = N_HEADS // N_KV_HEADS
D_ATT = N_HEADS * HEAD_DIM
D_KV = N_KV_HEADS * HEAD_DIM
WINDOW = 128
BLOCK = 128
N_BUCKETS = 32
MAX_DISTANCE = 128
D_CONV = D_MODEL // 2
D_FF = 4 * D_MODEL
N_MOD = 6
EPS = 1e-6
NEG = -1e30
LOG2E = np.float32(np.log2(np.e))

OFF_Q = 0
OFF_K = OFF_Q + D_ATT
OFF_V = OFF_K + D_KV
OFF_U = OFF_V + D_KV
OFF_GB = OFF_U + D_CONV
OFF_GC = OFF_GB + D_CONV
OFF_GA = OFF_GC + D_CONV
OFF_GCV = OFF_GA + D_MODEL
D_IN = OFF_GCV + D_MODEL

VMEM_LIMIT_BYTES = 56 * 1024 * 1024
VMEM_LIMIT_LARGE = 60 * 1024 * 1024

MM_BM = 1024
MM_BN = 1024
MM_BK = 4096
LANES = 128
NORM_ROWS = 512
NORM_MOD_ROWS = 512
NORM_HALF = 256
ATT_ROWS = 512
ATT_HEADS = 4
CONV_ROWS = 1024
CONV_COLS = 1024
HALO = 16
MOD_BN = 512
MOD_ROWS = 8


def _params(sem, vmem=VMEM_LIMIT_BYTES):
    return pltpu.CompilerParams(dimension_semantics=sem, vmem_limit_bytes=vmem)


def _mod_block(c_ref, w_ref, b_ref):
    c = c_ref[...]
    c_act = c * jax.nn.sigmoid(c)
    acc = jnp.dot(c_act.astype(BF16), w_ref[...].astype(BF16), preferred_element_type=F32)
    return acc + b_ref[...]


def _mod_kernel(c_ref, w_ref, b_ref, o_ref):
    o_ref[...] = _mod_block(c_ref, w_ref, b_ref)


def _mod_specs(d, bn, layer, first_block):
    in_specs = [
        pl.BlockSpec((None, d, bn), lambda i: (layer, 0, first_block + i)),
        pl.BlockSpec((None, 1, bn), lambda i: (layer, 0, first_block + i)),
    ]
    return in_specs, pl.BlockSpec((MOD_ROWS, bn), lambda i: (0, i))


def _modulation(c_pad, w_ada, b_ada, layer, ncols):
    d = w_ada.shape[1]
    in_specs, out_spec = _mod_specs(d, MOD_BN, layer, 0)
    return pl.pallas_call(
        _mod_kernel,
        grid=(ncols // MOD_BN,),
        in_specs=[pl.BlockSpec((MOD_ROWS, d), lambda i: (0, 0))] + in_specs,
        out_specs=out_spec,
        out_shape=jax.ShapeDtypeStruct((MOD_ROWS, ncols), F32),
        compiler_params=_params(("arbitrary",)),
        name="adaln_mod",
    )(c_pad, w_ada, b_ada)


def _inv_rms(x_ref, rstd_ref):
    d = x_ref.shape[1]
    for r in range(x_ref.shape[0] // NORM_HALF):
        rows = slice(r * NORM_HALF, (r + 1) * NORM_HALF)
        acc = jnp.zeros((NORM_HALF, LANES), F32)
        for c in range(d // LANES):
            x = x_ref[rows, c * LANES:(c + 1) * LANES]
            acc = acc + x * x
        mean_sq = jnp.sum(acc, axis=-1, keepdims=True) / d
        rstd_ref[rows, :] = jnp.broadcast_to(lax.rsqrt(mean_sq + EPS), (NORM_HALF, LANES))


def _norm_mod_kernel(x_ref, g_ref, scale_ref, shift_ref, o_ref, rstd_ref):
    _inv_rms(x_ref, rstd_ref)
    for r in range(x_ref.shape[0] // NORM_HALF):
        rows = slice(r * NORM_HALF, (r + 1) * NORM_HALF)
        for c in range(x_ref.shape[1] // LANES):
            cols = slice(c * LANES, (c + 1) * LANES)
            y = x_ref[rows, cols] * rstd_ref[rows, :]
            h = (y * g_ref[:, cols]) * (1.0 + scale_ref[:, cols]) + shift_ref[:, cols]
            o_ref[rows, cols] = h.astype(o_ref.dtype)


def _norm_kernel(x_ref, g_ref, o_ref, rstd_ref):
    _inv_rms(x_ref, rstd_ref)
    for c in range(x_ref.shape[1] // LANES):
        cols = slice(c * LANES, (c + 1) * LANES)
        y = x_ref[:, cols] * rstd_ref[...]
        o_ref[:, cols] = (y * g_ref[:, cols]).astype(o_ref.dtype)


def _norm_mod(x, g, scale, shift, seq):
    m, d = x.shape
    per_batch = seq // NORM_MOD_ROWS
    row = pl.BlockSpec((NORM_MOD_ROWS, d), lambda i: (i, 0))
    mod = pl.BlockSpec((None, 1, d), lambda i: (i // per_batch, 0, 0))
    return pl.pallas_call(
        _norm_mod_kernel,
        grid=(m // NORM_MOD_ROWS,),
        in_specs=[row, pl.BlockSpec((1, d), lambda i: (0, 0)), mod, mod],
        out_specs=row,
        out_shape=jax.ShapeDtypeStruct((m, d), BF16),
        scratch_shapes=[pltpu.VMEM((NORM_MOD_ROWS, LANES), F32)],
        compiler_params=_params(("arbitrary",)),
        name="norm_mod",
    )(x, g.reshape(1, d), scale, shift)


def _final_norm(x, g):
    m, d = x.shape
    row = pl.BlockSpec((NORM_ROWS, d), lambda i: (i, 0))
    return pl.pallas_call(
        _norm_kernel,
        grid=(m // NORM_ROWS,),
        in_specs=[row, pl.BlockSpec((1, d), lambda i: (0, 0))],
        out_specs=row,
        out_shape=jax.ShapeDtypeStruct((m, d), F32),
        scratch_shapes=[pltpu.VMEM((NORM_ROWS, LANES), F32)],
        compiler_params=_params(("arbitrary",)),
        name="final_norm",
    )(x, g.reshape(1, d))


def _mm_resident_kernel(*refs, npairs, nextra, nside, epilogue):
    a_refs = refs[:npairs]
    wchunk_refs = refs[npairs:2 * npairs]
    extra = refs[2 * npairs:2 * npairs + nextra]
    side_in = refs[2 * npairs + nextra:2 * npairs + nextra + nside]
    o_ref = refs[2 * npairs + nextra + nside]
    side_out = refs[2 * npairs + nextra + nside + 1:2 * npairs + nextra + 2 * nside + 1]
    wbuf_refs = refs[2 * npairs + nextra + 2 * nside + 1:]
    n = pl.program_id(0)
    i = pl.program_id(1)

    def convert_chunk():
        for wchunk, wbuf in zip(wchunk_refs, wbuf_refs):
            rows = wchunk.shape[0]
            wbuf[n % 2, pl.ds(pl.multiple_of(i * rows, rows), rows), :] = wchunk[...].astype(BF16)

    @pl.when(n == 0)
    def _():
        convert_chunk()

    @pl.when(n > 0)
    def _():
        convert_chunk()
        for src, dst in zip(side_in, side_out):
            dst[...] = src[...].astype(dst.dtype)
        slot = (n + 1) % 2
        accs = [jnp.dot(a[...], wbuf[slot], preferred_element_type=F32) for a, wbuf in zip(a_refs, wbuf_refs)]
        o_ref[...] = epilogue(accs, extra).astype(o_ref.dtype)


def _matmul_resident(pairs, layer, extras, epilogue, n_out, out_dtype, name, bn=MM_BN, convert=None,
                     vmem=VMEM_LIMIT_BYTES):
    m = pairs[0][0].shape[0]
    mt, nt = m // MM_BM, n_out // bn

    def at_tile(f):
        return lambda n, i: f(jnp.where(n > 0, i, 0), jnp.maximum(n - 1, 0))

    a_specs = [pl.BlockSpec((MM_BM, a.shape[1]), at_tile(lambda i, j: (i, 0))) for a, _ in pairs]
    w_specs = [pl.BlockSpec((None, w.shape[1] // mt, bn), lambda n, i: (layer, i, jnp.minimum(n, nt - 1)))
               for _, w in pairs]
    ex_specs = [pl.BlockSpec(bs, at_tile(f)) for _, bs, f in extras]
    out_specs = [pl.BlockSpec((MM_BM, bn), at_tile(lambda i, j: (i, j)))]
    out_shape = [jax.ShapeDtypeStruct((m, n_out), out_dtype)]
    side_specs, side_args = [], []
    if convert is not None:
        _, r, c = convert.shape
        rows = r // (nt * mt)
        side_specs = [pl.BlockSpec((None, rows, c), at_tile(lambda i, j: (layer, j * mt + i, 0)))]
        side_args = [convert]
        out_specs.append(pl.BlockSpec((rows, c), at_tile(lambda i, j: (j * mt + i, 0))))
        out_shape.append(jax.ShapeDtypeStruct((r, c), BF16))
    body = functools.partial(_mm_resident_kernel, npairs=len(pairs), nextra=len(extras), nside=len(side_args),
                             epilogue=epilogue)
    outs = pl.pallas_call(
        body,
        grid=(nt + 1, mt),
        in_specs=a_specs + w_specs + ex_specs + side_specs,
        out_specs=out_specs,
        out_shape=out_shape,
        scratch_shapes=[pltpu.VMEM((2, w.shape[1], bn), BF16) for _, w in pairs],
        compiler_params=_params(("arbitrary", "arbitrary"), vmem),
        name=name,
    )(*[a for a, _ in pairs], *[w for _, w in pairs], *[e for e, _, _ in extras], *side_args)
    return outs[0] if convert is None else outs


def _mm_ksplit_kernel(a_ref, w_ref, x_ref, gate_ref, o_ref, *, nk):
    k = pl.program_id(2)

    def part():
        return jnp.dot(a_ref[...], w_ref[...], preferred_element_type=F32)

    @pl.when(k == 0)
    def _():
        o_ref[...] = part()

    @pl.when((k > 0) & (k < nk - 1))
    def _():
        o_ref[...] += part()

    @pl.when(k == nk - 1)
    def _():
        o_ref[...] = x_ref[...] + gate_ref[...] * (o_ref[...] + part())


def _matmul_ksplit_residual(a, w, x, gate, gate_map, name):
    m, kdim = a.shape
    n_out = w.shape[1]
    nk = kdim // MM_BK
    tile = pl.BlockSpec((MM_BM, MM_BN), lambda i, j, k: (i, j))
    return pl.pallas_call(
        functools.partial(_mm_ksplit_kernel, nk=nk),
        grid=(m // MM_BM, n_out // MM_BN, nk),
        in_specs=[
            pl.BlockSpec((MM_BM, MM_BK), lambda i, j, k: (i, k)),
            pl.BlockSpec((MM_BK, MM_BN), lambda i, j, k: (k, j)),
            tile,
            pl.BlockSpec((None, 1, MM_BN), lambda i, j, k: gate_map(i, j)),
        ],
        out_specs=tile,
        out_shape=jax.ShapeDtypeStruct((m, n_out), F32),
        compiler_params=_params(("arbitrary", "arbitrary", "arbitrary")),
        name=name,
    )(a, w, x, gate)


def _ep_plain(accs, extra):
    return accs[0]


def _ep_relu2(accs, extra):
    return jnp.square(jnp.maximum(accs[0], 0.0))


def _ep_merge(accs, extra):
    glog_att, glog_conv, b_att, b_conv = extra
    g_att = jax.nn.sigmoid(glog_att[...].astype(F32) + b_att[...])
    g_conv = jax.nn.sigmoid(glog_conv[...].astype(F32) + b_conv[...])
    return g_att * accs[0] + g_conv * accs[1]


def _ep_residual(accs, extra):
    x, gate = extra
    return x[...] + gate[...] * accs[0]


def _bias_kernel(rel_ref, bucket_ref, o_ref):
    h = pl.program_id(0)
    bucket = bucket_ref[...]
    acc = jnp.full(bucket.shape, NEG, F32)
    for b in range(N_BUCKETS):
        acc = jnp.where(bucket == b, rel_ref[b, h] * LOG2E, acc)
    o_ref[...] = acc


def _t5_bucket(rel):
    nb = N_BUCKETS // 2
    max_exact = nb // 2
    ret = (rel > 0).astype(jnp.int32) * nb
    n = jnp.abs(rel)
    nf = jnp.maximum(n, 1).astype(F32)
    large = max_exact + (jnp.log(nf / max_exact) / np.float32(np.log(MAX_DISTANCE / max_exact))
                         * (nb - max_exact)).astype(jnp.int32)
    large = jnp.minimum(large, nb - 1)
    return ret + jnp.where(n < max_exact, n, large)


def _bias_table(rel_bias):
    q_idx = jnp.arange(BLOCK)[:, None]
    s_idx = jnp.arange(3 * BLOCK)[None, :]
    rel = s_idx - BLOCK - q_idx
    bucket = jnp.where(jnp.abs(rel) <= WINDOW, _t5_bucket(rel), -1).astype(jnp.int32)
    return pl.pallas_call(
        _bias_kernel,
        grid=(N_HEADS,),
        in_specs=[
            pl.BlockSpec(memory_space=pltpu.SMEM),
            pl.BlockSpec((BLOCK, 3 * BLOCK), lambda h: (0, 0)),
        ],
        out_specs=pl.BlockSpec((None, BLOCK, 3 * BLOCK), lambda h: (h, 0, 0)),
        out_shape=jax.ShapeDtypeStruct((N_HEADS, BLOCK, 3 * BLOCK), F32),
        compiler_params=_params(("arbitrary",)),
        name="rel_bias_table",
    )(rel_bias, bucket)


def _attn_kernel(sink_ref, q_ref, kc_ref, vc_ref, kp_ref, vp_ref, kn_ref, vn_ref, bias_ref, *rest,
                 tiles_per_seq):
    njobs = (len(rest) - 1) // 3
    o_ref = rest[2 * njobs + 1] if njobs else rest[0]
    for k in range(njobs):
        rest[2 * njobs + 2 + k][...] = _mod_block(rest[0], rest[1 + 2 * k], rest[2 + 2 * k])
    t = pl.program_id(0) % tiles_per_seq
    nsub = ATT_ROWS // BLOCK
    col = lax.broadcasted_iota(jnp.int32, (1, 3 * BLOCK), 1)
    head_row = lax.broadcasted_iota(jnp.int32, (ATT_HEADS * BLOCK, 1), 0)
    scale = np.float32(HEAD_DIM ** -0.5) * LOG2E

    def band(cur_ref, prev_ref, next_ref, j, cols):
        if j == 0:
            return jnp.concatenate([prev_ref[:, cols], cur_ref[0:2 * BLOCK, cols]], axis=0)
        if j == nsub - 1:
            return jnp.concatenate([cur_ref[(j - 1) * BLOCK:(j + 1) * BLOCK, cols], next_ref[:, cols]], axis=0)
        return cur_ref[(j - 1) * BLOCK:(j + 2) * BLOCK, cols]

    def dead_keys(j):
        if j == 0:
            return (col < BLOCK) & (t == 0)
        if j == nsub - 1:
            return (col >= 2 * BLOCK) & (t == tiles_per_seq - 1)
        return None

    def scores(j, kv, part):
        rows = slice(j * BLOCK, (j + 1) * BLOCK)
        heads = [kv * GROUP + part * ATT_HEADS + g for g in range(ATT_HEADS)]
        q = jnp.concatenate([q_ref[rows, h * HEAD_DIM:(h + 1) * HEAD_DIM] for h in heads], axis=0)
        kb = band(kc_ref, kp_ref, kn_ref, j, slice(kv * HEAD_DIM, (kv + 1) * HEAD_DIM))
        return lax.dot_general(q, kb, (((1,), (1,)), ((), ())), preferred_element_type=F32)

    def softmax(j, kv, part, s):
        heads = [kv * GROUP + part * ATT_HEADS + g for g in range(ATT_HEADS)]
        s = s * scale + bias_ref[kv, part * ATT_HEADS * BLOCK:(part + 1) * ATT_HEADS * BLOCK, :]
        dead = dead_keys(j)
        if dead is not None:
            s = jnp.where(dead, NEG, s)
        sink = jnp.full((ATT_HEADS * BLOCK, 1), sink_ref[heads[0]] * LOG2E, F32)
        for g in range(1, ATT_HEADS):
            sink = jnp.where(head_row >= g * BLOCK, sink_ref[heads[g]] * LOG2E, sink)
        mx = jnp.maximum(jnp.max(s, axis=-1, keepdims=True), sink)
        p = jnp.exp2(s - mx)
        denom = jnp.sum(p, axis=-1, keepdims=True) + jnp.exp2(sink - mx)
        return p.astype(BF16), denom

    def finish(j, kv, part, p, denom):
        rows = slice(j * BLOCK, (j + 1) * BLOCK)
        vb = band(vc_ref, vp_ref, vn_ref, j, slice(kv * HEAD_DIM, (kv + 1) * HEAD_DIM))
        o = jnp.dot(p, vb, preferred_element_type=F32) / denom
        for g in range(ATT_HEADS):
            h = kv * GROUP + part * ATT_HEADS + g
            o_ref[rows, h * HEAD_DIM:(h + 1) * HEAD_DIM] = o[g * BLOCK:(g + 1) * BLOCK].astype(o_ref.dtype)

    items = [(j, kv, part) for j in range(nsub) for kv in range(N_KV_HEADS) for part in range(GROUP // ATT_HEADS)]
    s_next = scores(*items[0])
    for idx, item in enumerate(items):
        s_cur = s_next
        if idx + 1 < len(items):
            s_next = scores(*items[idx + 1])
        p, denom = softmax(*item, s_cur)
        finish(*item, p, denom)


def _attention(z, bias_st, sink, seq, c_pad, w_ada, b_ada, mod_jobs):
    m = z.shape[0]
    steps = m // ATT_ROWS
    tiles_per_seq = seq // ATT_ROWS
    sub = ATT_ROWS // BLOCK
    last_blk = m // BLOCK - 1
    kcol, vcol = OFF_K // D_KV, OFF_V // D_KV
    cur = lambda c: pl.BlockSpec((ATT_ROWS, D_KV), lambda i: (i, c))
    prev = lambda c: pl.BlockSpec((BLOCK, D_KV), lambda i: (jnp.maximum(i * sub - 1, 0), c))
    nxt = lambda c: pl.BlockSpec((BLOCK, D_KV), lambda i: (jnp.minimum(i * sub + sub, last_blk), c))
    in_specs = [
        pl.BlockSpec(memory_space=pltpu.SMEM),
        pl.BlockSpec((ATT_ROWS, D_ATT), lambda i: (i, OFF_Q // D_ATT)),
        cur(kcol), cur(vcol), prev(kcol), prev(vcol), nxt(kcol), nxt(vcol),
        pl.BlockSpec((N_KV_HEADS, GROUP * BLOCK, 3 * BLOCK), lambda i: (0, 0, 0)),
    ]
    args = [sink, z, z, z, z, z, z, z, bias_st]
    out_specs = [pl.BlockSpec((ATT_ROWS, D_ATT), lambda i: (i, 0))]
    out_shape = [jax.ShapeDtypeStruct((m, D_ATT), BF16)]
    if mod_jobs:
        d = w_ada.shape[1]
        in_specs.append(pl.BlockSpec((MOD_ROWS, d), lambda i: (0, 0)))
        args.append(c_pad)
    for layer, col0, ncols in mod_jobs:
        bn = ncols // steps
        mod_in, mod_out = _mod_specs(d, bn, layer, col0 // bn)
        in_specs += mod_in
        args += [w_ada, b_ada]
        out_specs.append(mod_out)
        out_shape.append(jax.ShapeDtypeStruct((MOD_ROWS, ncols), F32))
    outs = pl.pallas_call(
        functools.partial(_attn_kernel, tiles_per_seq=tiles_per_seq),
        grid=(steps,),
        in_specs=in_specs,
        out_specs=out_specs,
        out_shape=out_shape,
        compiler_params=_params(("arbitrary",)),
        name="window_gqa",
    )(*args)
    return outs[0], outs[1:]


def _conv_kernel(u_ref, gb_ref, gc_ref, up_ref, gcp_ref, un_ref, gcn_ref, w_ref, o_ref, *, tiles_per_seq):
    t = pl.program_id(0) % tiles_per_seq
    p = u_ref[...].astype(F32) * gc_ref[...].astype(F32)
    first = (up_ref[...].astype(F32) * gcp_ref[...].astype(F32))[HALO - 1:HALO]
    last = (un_ref[...].astype(F32) * gcn_ref[...].astype(F32))[0:1]
    first = jnp.where(t == 0, 0.0, first)
    last = jnp.where(t == tiles_per_seq - 1, 0.0, last)
    row = lax.broadcasted_iota(jnp.int32, (CONV_ROWS, 1), 0)
    before = jnp.where(row == 0, first, pltpu.roll(p, 1, axis=0))
    after = jnp.where(row == CONV_ROWS - 1, last, pltpu.roll(p, CONV_ROWS - 1, axis=0))
    y = w_ref[0:1, :] * before + w_ref[1:2, :] * p + w_ref[2:3, :] * after
    o_ref[...] = (gb_ref[...].astype(F32) * y).astype(o_ref.dtype)


def _conv_gate(z, conv_w, seq):
    m = z.shape[0]
    tiles_per_seq = seq // CONV_ROWS
    per = CONV_ROWS // HALO
    last_blk = m // HALO - 1
    main = lambda off: pl.BlockSpec((CONV_ROWS, CONV_COLS), lambda i, c: (i, off // CONV_COLS + c))
    prev = lambda off: pl.BlockSpec((HALO, CONV_COLS),
                                    lambda i, c: (jnp.maximum(i * per - 1, 0), off // CONV_COLS + c))
    nxt = lambda off: pl.BlockSpec((HALO, CONV_COLS),
                                   lambda i, c: (jnp.minimum(i * per + per, last_blk), off // CONV_COLS + c))
    return pl.pallas_call(
        functools.partial(_conv_kernel, tiles_per_seq=tiles_per_seq),
        grid=(m // CONV_ROWS, D_CONV // CONV_COLS),
        in_specs=[main(OFF_U), main(OFF_GB), main(OFF_GC), prev(OFF_U), prev(OFF_GC), nxt(OFF_U), nxt(OFF_GC),
                  pl.BlockSpec((3, CONV_COLS), lambda i, c: (0, c))],
        out_specs=pl.BlockSpec((CONV_ROWS, CONV_COLS), lambda i, c: (i, c)),
        out_shape=jax.ShapeDtypeStruct((m, D_CONV), BF16),
        compiler_params=_params(("arbitrary", "arbitrary")),
        name="conv_gate",
    )(z, z, z, z, z, z, z, conv_w)


def kernel(x, c, rel_bias, norm1_g, norm2_g, w_ada, b_ada, w_in, b_merge, sink, conv_w,
           w_att_proj, w_conv_proj, w_out, w_mlp_in, w_mlp_out, final_g):
    bsz, seq, d = x.shape
    depth = w_ada.shape[0]
    m = bsz * seq
    tiles_per_batch = seq // MM_BM
    xf = x.reshape(m, d)

    c_pad = jnp.zeros((MOD_ROWS, d), F32).at[:bsz].set(c)
    b_ada3 = b_ada.reshape(depth, 1, N_MOD * d)
    early = 2 * d
    late = (N_MOD - 2) * d
    early_rows = _modulation(c_pad, w_ada, b_ada3, 0, early)
    bias_st = _bias_table(rel_bias).reshape(N_KV_HEADS, GROUP * BLOCK, 3 * BLOCK)

    def gate_map(i, j):
        return (i // tiles_per_batch, 0, j)

    for l in range(depth):
        mods = early_rows[:bsz].reshape(bsz, 2, 1, d)
        shift1, scale1 = mods[:, 0], mods[:, 1]
        b_m = b_merge[l].reshape(1, 2 * d)

        h = _norm_mod(xf, norm1_g[l], scale1, shift1, seq)
        z = _matmul_resident([(h, w_in)], l, [], _ep_plain, D_IN, BF16, "in_proj")
        mod_jobs = [(l, early, late)] + ([(l + 1, 0, early)] if l + 1 < depth else [])
        att, mod_parts = _attention(z, bias_st, sink[l], seq, c_pad, w_ada, b_ada3, mod_jobs)
        mods = mod_parts[0][:bsz].reshape(bsz, N_MOD - 2, 1, d)
        gate1, shift2, scale2, gate2 = [mods[:, t] for t in range(N_MOD - 2)]
        if l + 1 < depth:
            early_rows = mod_parts[1]
        bm = _conv_gate(z, conv_w[l], seq)
        merged = _matmul_resident(
            [(att, w_att_proj), (bm, w_conv_proj)], l,
            [(z, (MM_BM, MM_BN), lambda i, j: (i, OFF_GA // MM_BN + j)),
             (z, (MM_BM, MM_BN), lambda i, j: (i, OFF_GCV // MM_BN + j)),
             (b_m, (1, MM_BN), lambda i, j: (0, j)),
             (b_m, (1, MM_BN), lambda i, j: (0, d // MM_BN + j))],
            _ep_merge, d, BF16, "branch_merge")
        xf = _matmul_resident([(merged, w_out)], l,
                              [(xf, (MM_BM, MM_BN), lambda i, j: (i, j)), (gate1, (None, 1, MM_BN), gate_map)],
                              _ep_residual, d, F32, "out_proj", vmem=VMEM_LIMIT_LARGE)

        h2 = _norm_mod(xf, norm2_g[l], scale2, shift2, seq)
        a, w_mlp_out_bf16 = _matmul_resident([(h2, w_mlp_in)], l, [], _ep_relu2, D_FF, BF16, "mlp_in",
                                             convert=w_mlp_out)
        xf = _matmul_ksplit_residual(a, w_mlp_out_bf16, xf, gate2, gate_map, "mlp_out")

    return _final_norm(xf, final_g).reshape(bsz, seq, d)
```
